```python
import functools
import jax, jax.numpy as jnp
from jax import lax
import numpy as np

D_MODEL = 2048
BATCH = 1
SEQ = 8192
DEPTH = 2
DEC_BATCH = 32
DEC_SEQ = 8
PAST_LEN = 8192
PAGE_SIZE = 128

H_R = 8
DK_R = 128
DV_R = 128
W_RK = H_R * DK_R
W_R = H_R * DV_R
HGRN_CHUNK = 64
H_A = 8
HD_A = 128
W_A = H_A * HD_A
MOBA_BLOCK = 256
MOBA_TOPK = 3
Q_BLOCK = 128
ROPE_THETA = 10000.0
N_MEM = 256
H_M = 4
HD_M = 128
W_M = H_M * HD_M
D_FF = 5632
EPS = 1e-6
N_IN = 2 * W_RK + 2 * W_R + 3 * W_A + 2 * D_MODEL
IN_SPLITS = (W_RK, 2 * W_RK, 2 * W_RK + W_R, 2 * W_RK + 2 * W_R, 2 * W_RK + 2 * W_R + W_A, 2 * W_RK + 2 * W_R + 2 * W_A, 2 * W_RK + 2 * W_R + 3 * W_A, 2 * W_RK + 2 * W_R + 3 * W_A + D_MODEL)

kernel_name = "hgrn2_moba_gated_hybrid_step"


def rmsnorm(x, g):
    xf = x.astype(jnp.float32)
    y = xf * lax.rsqrt(jnp.mean(xf * xf, axis=-1, keepdims=True) + EPS)
    return (y * g.astype(jnp.float32)).astype(x.dtype)


def swiglu(x, w_gate, w_up, w_down):
    return (jax.nn.silu(x @ w_gate) * (x @ w_up)) @ w_down


def rope(x, pos):
    half = x.shape[-1] // 2
    inv_freq = ROPE_THETA ** (-jnp.arange(half, dtype=jnp.float32) / half)
    ang = pos.astype(jnp.float32)[:, None] * inv_freq[None, :]
    cos = jnp.cos(ang)[:, None, :]
    sin = jnp.sin(ang)[:, None, :]
    xf = x.astype(jnp.float32)
    x1, x2 = xf[..., :half], xf[..., half:]
    return jnp.concatenate([x1 * cos - x2 * sin, x2 * cos + x1 * sin], axis=-1).astype(x.dtype)


def hgrn2_scan(q, k, g, v, s0):
    b, t, h, _ = q.shape
    c = HGRN_CHUNK if t % HGRN_CHUNK == 0 else t
    n = t // c

    def to_chunks(a):
        return a.astype(jnp.float32).reshape(b, n, c, h, a.shape[-1]).transpose(1, 0, 3, 2, 4)

    causal = jnp.tril(jnp.ones((c, c), dtype=bool))[:, :, None]

    def step(s, xs):
        qi, ki, gi, vi = xs
        gc = jnp.cumsum(gi, axis=2)
        diff = gc[:, :, :, None, :] - gc[:, :, None, :, :]
        decay = jnp.exp(jnp.where(causal, diff, -jnp.inf))
        a = jnp.einsum("bhtd,bhsd,bhtsd->bhts", qi, ki, decay)
        o = jnp.einsum("bhts,bhsv->bhtv", a, vi) + jnp.einsum("bhtd,bhdv->bhtv", qi * jnp.exp(gc), s)
        g_end = gc[:, :, -1:, :]
        s_new = jnp.exp(g_end[:, :, 0, :])[..., None] * s + jnp.einsum("bhsd,bhsv->bhdv", ki * jnp.exp(g_end - gc), vi)
        return s_new, o

    s_fin, o = lax.scan(step, s0.astype(jnp.float32), (to_chunks(q), to_chunks(k), to_chunks(g), to_chunks(v)))
    return o.transpose(1, 0, 3, 2, 4).reshape(b, t, h, -1), s_fin


def hgrn2_branch(q_r, f_r, i_r, og_r, lb, gnorm, s0):
    b, t, _ = q_r.shape
    q = jax.nn.silu(q_r).reshape(b, t, H_R, DK_R) * (DK_R ** -0.5)
    log_f = jnp.logaddexp(jnp.log(lb), jnp.log1p(-lb) + jax.nn.log_sigmoid(f_r.astype(jnp.float32)))
    k = -jnp.expm1(log_f)
    o, s_new = hgrn2_scan(q, k.reshape(b, t, H_R, DK_R), log_f.reshape(b, t, H_R, DK_R), i_r.reshape(b, t, H_R, DV_R), s0)
    o = rmsnorm(o, gnorm).reshape(b, t, W_R).astype(q_r.dtype) * jax.nn.silu(og_r)
    return o, s_new


def moba_len(t):
    return max(-(-t // MOBA_BLOCK), MOBA_TOPK) * MOBA_BLOCK


def block_means(k):
    shp = k.shape[:-3] + (k.shape[-3] // MOBA_BLOCK, MOBA_BLOCK) + k.shape[-2:]
    return jnp.mean(k.reshape(shp).astype(jnp.float32), axis=-3)


def moba_queries(q, q_pos, k_seq, v_seq, kmean):
    tq = q.shape[0]
    n_blk = k_seq.shape[0] // MOBA_BLOCK
    kb = k_seq.reshape(n_blk, MOBA_BLOCK, H_A, HD_A)
    vb = v_seq.reshape(n_blk, MOBA_BLOCK, H_A, HD_A)
    qf = q.astype(jnp.float32)
    q_blk = q_pos // MOBA_BLOCK
    s_gate = jnp.einsum("thd,nhd->thn", qf, kmean)
    past = jnp.arange(n_blk)[None, None, :] < q_blk[:, None, None]
    _, idx = lax.top_k(jnp.where(past, s_gate, -jnp.inf), MOBA_TOPK)
    valid = idx < q_blk[:, None, None]
    h_idx = jnp.arange(H_A)[None, :, None]
    k_sel = kb[idx, :, h_idx, :].astype(jnp.float32)
    v_sel = vb[idx, :, h_idx, :].astype(jnp.float32)
    own = q_blk[0]
    k_own = lax.dynamic_index_in_dim(kb, own, axis=0, keepdims=False).astype(jnp.float32)
    v_own = lax.dynamic_index_in_dim(vb, own, axis=0, keepdims=False).astype(jnp.float32)
    own_pos = own * MOBA_BLOCK + jnp.arange(MOBA_BLOCK)
    scale = HD_A ** -0.5
    s_sel = jnp.einsum("thd,thjkd->thjk", qf, k_sel) * scale
    s_sel = jnp.where(valid[..., None], s_sel, -jnp.inf).reshape(tq, H_A, MOBA_TOPK * MOBA_BLOCK)
    s_own = jnp.einsum("thd,khd->thk", qf, k_own) * scale
    s_own = jnp.where((own_pos[None, :] <= q_pos[:, None])[:, None, :], s_own, -jnp.inf)
    p = jax.nn.softmax(jnp.concatenate([s_sel, s_own], axis=-1), axis=-1)
    p_sel = p[..., :MOBA_TOPK * MOBA_BLOCK].reshape(tq, H_A, MOBA_TOPK, MOBA_BLOCK)
    p_own = p[..., MOBA_TOPK * MOBA_BLOCK:]
    return jnp.einsum("thjk,thjkd->thd", p_sel, v_sel) + jnp.einsum("thk,khd->thd", p_own, v_own)


def moba_prompt(q, k, v):
    b, t = q.shape[:2]
    pad = ((0, 0), (0, moba_len(t) - t), (0, 0), (0, 0))
    kp = jnp.pad(k, pad)
    vp = jnp.pad(v, pad)
    kmean = block_means(kp)

    def one_block(c):
        start = c * Q_BLOCK
        qc = lax.dynamic_slice_in_dim(q, start, Q_BLOCK, axis=1)
        pc = start + jnp.arange(Q_BLOCK, dtype=jnp.int32)
        return jax.vmap(moba_queries, in_axes=(0, None, 0, 0, 0))(qc, pc, kp, vp, kmean)

    o = lax.map(one_block, jnp.arange(t // Q_BLOCK, dtype=jnp.int32))
    return o.transpose(1, 0, 2, 3, 4).reshape(b, t, W_A)


def moba_sample(q, k, v, pool_k, pool_v, page_table):
    t = q.shape[1]
    length = moba_len(PAST_LEN + t)
    pos = PAST_LEN + jnp.arange(t, dtype=jnp.int32)

    def one_seq(xs):
        q_b, k_b, v_b, pages = xs
        pad = ((0, length - PAST_LEN - t), (0, 0), (0, 0))
        k_all = jnp.pad(jnp.concatenate([pool_k[pages].reshape(-1, H_A, HD_A), k_b.astype(pool_k.dtype)], axis=0), pad)
        v_all = jnp.pad(jnp.concatenate([pool_v[pages].reshape(-1, H_A, HD_A), v_b.astype(pool_v.dtype)], axis=0), pad)
        return moba_queries(q_b, pos, k_all, v_all, block_means(k_all))

    o = lax.map(one_seq, (q, k, v, page_table))
    return o.reshape(q.shape[0], t, W_A)


def token_mixing(u, pos, s0, attend, lp):
    b, t, _ = u.shape
    proj = u @ lp["w_in"]
    q_r, f_r, i_r, og_r, q_a, k_a, v_a, gate_r, gate_a = jnp.split(proj, IN_SPLITS, axis=-1)
    o_r, s_new = hgrn2_branch(q_r, f_r, i_r, og_r, lp["lb"], lp["hgrn_gnorm"], s0)
    q_a = rope(q_a.reshape(b, t, H_A, HD_A), pos)
    k_a = rope(k_a.reshape(b, t, H_A, HD_A), pos)
    v_a = v_a.reshape(b, t, H_A, HD_A)
    o_a = attend(q_a, k_a, v_a).astype(u.dtype)
    merged = jax.nn.sigmoid(gate_r) * (o_r @ lp["w_branch_r"]) + jax.nn.sigmoid(gate_a) * (o_a @ lp["w_branch_a"])
    return merged @ lp["w_out"], k_a, v_a, s_new


def memory_kv(mem, g, w_ck, w_cv):
    b = mem.shape[0]
    m = rmsnorm(mem, g)
    return (m @ w_ck).reshape(b, N_MEM, H_M, HD_M), (m @ w_cv).reshape(b, N_MEM, H_M, HD_M)


def cross_attend(u, mem_k, mem_v, w_cq, w_co):
    b, t, _ = u.shape
    q = (u @ w_cq).reshape(b, t, H_M, HD_M).astype(jnp.float32)
    s = jnp.einsum("bthd,bmhd->bhtm", q, mem_k.astype(jnp.float32)) * (HD_M ** -0.5)
    p = jax.nn.softmax(s, axis=-1)
    o = jnp.einsum("bhtm,bmhd->bthd", p, mem_v.astype(jnp.float32)).astype(u.dtype).reshape(b, t, W_M)
    return o @ w_co


def trunk_layer(x, pos, s0, mem_k, mem_v, attend, lp):
    u = rmsnorm(x, lp["norm_pre"][0])
    x = x + 0.5 * rmsnorm(swiglu(u, lp["ffn_gate"][0], lp["ffn_up"][0], lp["ffn_down"][0]), lp["norm_post"][0])
    u = rmsnorm(x, lp["norm_pre"][1])
    mix, k_a, v_a, s_new = token_mixing(u, pos, s0, attend, lp)
    x = x + rmsnorm(mix, lp["norm_post"][1])
    u = rmsnorm(x, lp["norm_pre"][2])
    x = x + rmsnorm(cross_attend(u, mem_k, mem_v, lp["w_cq"], lp["w_co"]), lp["norm_post"][2])
    u = rmsnorm(x, lp["norm_pre"][3])
    x = x + 0.5 * rmsnorm(swiglu(u, lp["ffn_gate"][1], lp["ffn_up"][1], lp["ffn_down"][1]), lp["norm_post"][3])
    return x, k_a, v_a, s_new


def setup_inputs(seed: int = 0) -> dict:
    key = jax.random.key(seed)
    keys = iter(jax.random.split(key, 32))

    def nrm(shape, scale):
        return jax.random.normal(next(keys), shape, jnp.float32) * scale

    def gain(shape):
        return 1.0 + nrm(shape, 0.05)

    n_pages = PAST_LEN // PAGE_SIZE
    n_used = DEC_BATCH * n_pages
    n_pool = n_used + max(1, n_used // 4)
    perm = jax.random.permutation(next(keys), n_pool)
    page_table = perm[:n_used].reshape(DEC_BATCH, n_pages).astype(jnp.int32)
    return {
        "x_prompt": nrm((BATCH, SEQ, D_MODEL), 1.0),
        "x_sample": nrm((DEC_BATCH, DEC_SEQ, D_MODEL), 1.0),
        "cache_k": nrm((DEPTH, n_pool, PAGE_SIZE, H_A, HD_A), 1.0),
        "cache_v": nrm((DEPTH, n_pool, PAGE_SIZE, H_A, HD_A), 1.0),
        "state_hgrn": nrm((DEPTH, DEC_BATCH, H_R, DK_R, DV_R), 0.5),
        "cache_mem_k": nrm((DEPTH, DEC_BATCH, N_MEM, H_M, HD_M), 1.0),
        "cache_mem_v": nrm((DEPTH, DEC_BATCH, N_MEM, H_M, HD_M), 1.0),
        "page_table": page_table,
        "mem_prompt": nrm((BATCH, N_MEM, D_MODEL), 1.0),
        "norm_pre": gain((DEPTH, 4, D_MODEL)),
        "norm_post": gain((DEPTH, 4, D_MODEL)),
        "ffn_gate": nrm((DEPTH, 2, D_MODEL, D_FF), D_MODEL ** -0.5),
        "ffn_up": nrm((DEPTH, 2, D_MODEL, D_FF), D_MODEL ** -0.5),
        "ffn_down": nrm((DEPTH, 2, D_FF, D_MODEL), D_FF ** -0.5),
        "w_in": nrm((DEPTH, D_MODEL, N_IN), D_MODEL ** -0.5),
        "lower_bounds": nrm((DEPTH, W_RK), 0.5),
        "hgrn_gnorm": gain((DEPTH, H_R, DV_R)),
        "w_branch_r": nrm((DEPTH, W_R, D_MODEL), W_R ** -0.5),
        "w_branch_a": nrm((DEPTH, W_A, D_MODEL), W_A ** -0.5),
        "w_out": nrm((DEPTH, D_MODEL, D_MODEL), D_MODEL ** -0.5),
        "mem_norm": gain((DEPTH, D_MODEL)),
        "w_cq": nrm((DEPTH, D_MODEL, W_M), D_MODEL ** -0.5),
        "w_ck": nrm((DEPTH, D_MODEL, W_M), D_MODEL ** -0.5),
        "w_cv": nrm((DEPTH, D_MODEL, W_M), D_MODEL ** -0.5),
        "w_co": nrm((DEPTH, W_M, D_MODEL), W_M ** -0.5),
    }


def reference(x_prompt, x_sample, cache_k, cache_v, state_hgrn, cache_mem_k, cache_mem_v, page_table, mem_prompt,
              norm_pre, norm_post, ffn_gate, ffn_up, ffn_down, w_in, lower_bounds, hgrn_gnorm,
              w_branch_r, w_branch_a, w_out, mem_norm, w_cq, w_ck, w_cv, w_co):
    lb_cum = jnp.cumsum(jax.nn.softmax(lower_bounds.astype(jnp.float32), axis=0), axis=0)
    lb_all = lb_cum - lb_cum[0]
    pos_p = jnp.arange(SEQ, dtype=jnp.int32)
    pos_s = PAST_LEN + jnp.arange(DEC_SEQ, dtype=jnp.int32)
    s0_p = jnp.zeros((BATCH, H_R, DK_R, DV_R), jnp.float32)
    xp, xs = x_prompt, x_sample
    kp_l, vp_l, sp_l, mkp_l, mvp_l, ks_l, vs_l, ss_l = [], [], [], [], [], [], [], []
    for l in range(DEPTH):
        lp = {
            "norm_pre": norm_pre[l], "norm_post": norm_post[l],
            "ffn_gate": ffn_gate[l], "ffn_up": ffn_up[l], "ffn_down": ffn_down[l],
            "w_in": w_in[l], "lb": lb_all[l], "hgrn_gnorm": hgrn_gnorm[l],
            "w_branch_r": w_branch_r[l], "w_branch_a": w_branch_a[l], "w_out": w_out[l],
            "w_cq": w_cq[l], "w_co": w_co[l],
        }
        mk_p, mv_p = memory_kv(mem_prompt, mem_norm[l], w_ck[l], w_cv[l])
        xp, k_p, v_p, s_p = trunk_layer(xp, pos_p, s0_p, mk_p, mv_p, moba_prompt, lp)
        attend_s = functools.partial(moba_sample, pool_k=cache_k[l], pool_v=cache_v[l], page_table=page_table)
        xs, k_s, v_s, s_s = trunk_layer(xs, pos_s, state_hgrn[l], cache_mem_k[l], cache_mem_v[l], attend_s, lp)
        kp_l.append(k_p)
        vp_l.append(v_p)
        sp_l.append(s_p)
        mkp_l.append(mk_p)
        mvp_l.append(mv_p)
        ks_l.append(k_s)
        vs_l.append(v_s)
        ss_l.append(s_s)
    return (xp, xs, jnp.stack(kp_l), jnp.stack(vp_l), jnp.stack(sp_l), jnp.stack(mkp_l), jnp.stack(mvp_l), jnp.stack(ks_l), jnp.stack(vs_l), jnp.stack(ss_l))
```

```python
import functools

import numpy as np
import jax
import jax.numpy as jnp
from jax import lax
from jax.experimental import pallas as pl
from jax.experimental.pallas import tpu as pltpu

F32 = jnp.float32
BF16 = jnp.bfloat16

MOBA_BLOCK = 256
MOBA_TOPK = 3
ROPE_THETA = 10000.0
EPS = 1e-6
HEAD = 128
LANES = 128
NEG = -1e30
VMEM_LIMIT = 56 * 1024 * 1024


def _cparams(sem, vmem=VMEM_LIMIT):
    return pltpu.CompilerParams(dimension_semantics=sem, vmem_limit_bytes=vmem)


def _pick(n, cands):
    for c in cands:
        if n % c == 0:
            return c
    raise ValueError(f"no tile in {cands} divides {n}")


def _wspec(lead, shape, index_map, resident=False):
    full_shape = (None,) * len(lead) + tuple(shape)
    full_map = lambda *a: tuple(lead) + tuple(index_map(*a))
    if resident:
        return pl.BlockSpec(full_shape, full_map, pipeline_mode=pl.Buffered(1))
    return pl.BlockSpec(full_shape, full_map)


def _dot(a, b):
    return jnp.dot(a, b, preferred_element_type=F32)


def _dot_nt(a, b):
    return lax.dot_general(a, b, (((1,), (1,)), ((), ())), preferred_element_type=F32)


def _split2(a):
    hi = a.astype(BF16)
    lo = (a - hi.astype(F32)).astype(BF16)
    return hi, lo


def _dot3_nt(a, b):
    ah, al = _split2(a)
    bh, bl = _split2(b)
    return _dot_nt(ah, bh) + _dot_nt(ah, bl) + _dot_nt(al, bh)


def _dot3(a, b):
    ah, al = _split2(a)
    bh, bl = _split2(b)
    return _dot(ah, bh) + _dot(ah, bl) + _dot(al, bh)


def _sigmoid(x):
    return 1.0 / (1.0 + jnp.exp(-x))


def _silu(x):
    return x * _sigmoid(x)


def _rms(y, g):
    ms = jnp.mean(y * y, axis=-1, keepdims=True)
    return y * lax.rsqrt(ms + EPS) * g


def _norm_kernel(x_ref, g_ref, u_ref):
    u_ref[...] = _rms(x_ref[...], g_ref[...]).astype(BF16)


def _norm_cast(x, g, bm):
    m, d = x.shape
    return pl.pallas_call(
        _norm_kernel,
        grid=(m // bm,),
        in_specs=[pl.BlockSpec((bm, d), lambda i: (i, 0)), pl.BlockSpec((1, d), lambda i: (0, 0))],
        out_specs=pl.BlockSpec((bm, d), lambda i: (i, 0)),
        out_shape=jax.ShapeDtypeStruct((m, d), BF16),
        compiler_params=_cparams(("parallel",)),
        name="norm_cast",
    )(x, g)


def _ffn_up_kernel(u_ref, wg_ref, wu_ref, h_ref):
    u = u_ref[...]
    h_ref[...] = (_silu(_dot(u, wg_ref[...])) * _dot(u, wu_ref[...])).astype(BF16)


def _ffn_up(u, wg, wu, lead, bm, bn):
    m, d = u.shape
    f = wg.shape[-1]
    return pl.pallas_call(
        _ffn_up_kernel,
        grid=(f // bn, m // bm),
        in_specs=[pl.BlockSpec((bm, d), lambda n, i: (i, 0)),
                  _wspec(lead, (d, bn), lambda n, i: (0, n)),
                  _wspec(lead, (d, bn), lambda n, i: (0, n))],
        out_specs=pl.BlockSpec((bm, bn), lambda n, i: (i, n)),
        out_shape=jax.ShapeDtypeStruct((m, f), BF16),
        compiler_params=_cparams(("parallel", "parallel")),
        name="ffn_up",
    )(u, wg, wu)


def _proj_res_kernel(a_ref, w_ref, x_ref, gp_ref, gn_ref, xo_ref, uo_ref, *, scale):
    y = _dot(a_ref[...].astype(BF16), w_ref[...])
    xn = x_ref[...] + scale * _rms(y, gp_ref[...])
    xo_ref[...] = xn
    uo_ref[...] = _rms(xn, gn_ref[...]).astype(BF16)


def _proj_res(a, w, lead, x, g_post, g_next, scale, bm):
    m, k = a.shape
    d = w.shape[-1]
    return pl.pallas_call(
        functools.partial(_proj_res_kernel, scale=scale),
        grid=(m // bm,),
        in_specs=[pl.BlockSpec((bm, k), lambda i: (i, 0)),
                  _wspec(lead, (k, d), lambda i: (0, 0), resident=True),
                  pl.BlockSpec((bm, d), lambda i: (i, 0)),
                  pl.BlockSpec((1, d), lambda i: (0, 0)),
                  pl.BlockSpec((1, d), lambda i: (0, 0))],
        out_specs=[pl.BlockSpec((bm, d), lambda i: (i, 0)), pl.BlockSpec((bm, d), lambda i: (i, 0))],
        out_shape=[jax.ShapeDtypeStruct((m, d), F32), jax.ShapeDtypeStruct((m, d), BF16)],
        compiler_params=_cparams(("parallel",)),
        name="proj_res",
    )(a, w, x, g_post, g_next)


def _win_hgrn_kernel(u_ref, w_ref, lb_ref, o_ref, *, layer):
    n = pl.program_id(0)
    acc = _dot(u_ref[...], w_ref[...])

    @pl.when(n == 0)
    def _():
        o_ref[...] = _silu(acc) * (HEAD ** -0.5)

    @pl.when(n == 1)
    def _():
        lbs = lb_ref[...]
        e = jnp.exp(lbs - jnp.max(lbs, axis=0, keepdims=True))
        sm = e / jnp.sum(e, axis=0, keepdims=True)
        cum = sm[0:1, :]
        for i in range(1, layer + 1):
            cum = cum + sm[i:i + 1, :]
        lb = cum - sm[0:1, :]
        ls = jnp.minimum(acc, 0.0) - jnp.log1p(jnp.exp(-jnp.abs(acc)))
        a = jnp.log(lb)
        b = jnp.log1p(-lb) + ls
        o_ref[...] = jnp.maximum(a, b) + jnp.log1p(jnp.exp(-jnp.abs(a - b)))

    @pl.when(n == 2)
    def _():
        o_ref[...] = acc

    @pl.when(n == 3)
    def _():
        o_ref[...] = _silu(acc)


def _win_hgrn(u, w_in, lower_bounds, layer, bm):
    m, d = u.shape
    w = 8 * HEAD
    depth = lower_bounds.shape[0]
    return pl.pallas_call(
        functools.partial(_win_hgrn_kernel, layer=layer),
        grid=(4, m // bm),
        in_specs=[pl.BlockSpec((bm, d), lambda n, i: (i, 0)),
                  _wspec((layer,), (d, w), lambda n, i: (0, n)),
                  pl.BlockSpec((depth, w), lambda n, i: (0, 0))],
        out_specs=pl.BlockSpec((bm, w), lambda n, i: (i, n)),
        out_shape=jax.ShapeDtypeStruct((m, 4 * w), F32),
        compiler_params=_cparams(("parallel", "parallel")),
        name="win_hgrn",
    )(u, w_in, lower_bounds)


def _win_qk_kernel(u_ref, w_ref, cos_ref, sin_ref, o_ref, *, heads):
    acc = _dot(u_ref[...], w_ref[...])
    c = cos_ref[...]
    s = sin_ref[...]
    for h in range(heads):
        xh = acc[:, h * HEAD:(h + 1) * HEAD]
        o_ref[:, h * HEAD:(h + 1) * HEAD] = xh * c + pltpu.roll(xh, HEAD // 2, 1) * s


def _win_qk(u, w_in, layer, cos_t, sin_t, col_blk, heads, bm):
    m, d = u.shape
    w = heads * HEAD
    return pl.pallas_call(
        functools.partial(_win_qk_kernel, heads=heads),
        grid=(2, m // bm),
        in_specs=[pl.BlockSpec((bm, d), lambda n, i: (i, 0)),
                  _wspec((layer,), (d, w), lambda n, i: (0, col_blk + n)),
                  pl.BlockSpec((bm, HEAD), lambda n, i: (i, 0)),
                  pl.BlockSpec((bm, HEAD), lambda n, i: (i, 0))],
        out_specs=pl.BlockSpec((bm, w), lambda n, i: (i, n)),
        out_shape=jax.ShapeDtypeStruct((m, 2 * w), F32),
        compiler_params=_cparams(("parallel", "parallel")),
        name="win_qk",
    )(u, w_in, cos_t, sin_t)


def _win_v_kernel(u_ref, w_ref, v_ref, vt_ref):
    acc = _dot(u_ref[...], w_ref[...])
    v_ref[...] = acc
    vt_ref[...] = acc.T.astype(BF16)


def _win_v(u, w_in, layer, col_blk, heads, bm):
    m, d = u.shape
    w = heads * HEAD
    return pl.pallas_call(
        _win_v_kernel,
        grid=(m // bm,),
        in_specs=[pl.BlockSpec((bm, d), lambda i: (i, 0)),
                  _wspec((layer,), (d, w), lambda i: (0, col_blk))],
        out_specs=[pl.BlockSpec((bm, w), lambda i: (i, 0)), pl.BlockSpec((w, bm), lambda i: (0, i))],
        out_shape=[jax.ShapeDtypeStruct((m, w), F32), jax.ShapeDtypeStruct((w, m), BF16)],
        compiler_params=_cparams(("parallel",)),
        name="win_v",
    )(u, w_in)


def _win_gate_kernel(u_ref, w_ref, o_ref):
    o_ref[...] = _sigmoid(_dot(u_ref[...], w_ref[...])).astype(BF16)


def _win_gates(u, w_in, layer, col_blk, n_blk, bm):
    m, d = u.shape
    w = 8 * HEAD
    return pl.pallas_call(
        _win_gate_kernel,
        grid=(n_blk, m // bm),
        in_specs=[pl.BlockSpec((bm, d), lambda n, i: (i, 0)),
                  _wspec((layer,), (d, w), lambda n, i: (0, col_blk + n))],
        out_specs=pl.BlockSpec((bm, w), lambda n, i: (i, n)),
        out_shape=jax.ShapeDtypeStruct((m, n_blk * w), BF16),
        compiler_params=_cparams(("parallel", "parallel")),
        name="win_gates",
    )(u, w_in)


def _scaled_proj_kernel(u_ref, w_ref, o_ref, *, scale):
    o_ref[...] = _dot(u_ref[...], w_ref[...]) * scale


def _scaled_proj(u, w, lead, scale, bm):
    m, d = u.shape
    n = w.shape[-1]
    return pl.pallas_call(
        functools.partial(_scaled_proj_kernel, scale=scale),
        grid=(m // bm,),
        in_specs=[pl.BlockSpec((bm, d), lambda i: (i, 0)),
                  _wspec(lead, (d, n), lambda i: (0, 0), resident=True)],
        out_specs=pl.BlockSpec((bm, n), lambda i: (i, 0)),
        out_shape=jax.ShapeDtypeStruct((m, n), F32),
        compiler_params=_cparams(("parallel",)),
        name="scaled_proj",
    )(u, w)


def _memkv_kernel(mem_ref, g_ref, wk_ref, wv_ref, k_ref, v_ref):
    mn = _rms(mem_ref[...], g_ref[...]).astype(BF16)
    k_ref[...] = _dot(mn, wk_ref[...])
    v_ref[...] = _dot(mn, wv_ref[...])


def _memkv(mem, g, wk, wv, layer):
    n, d = mem.shape
    w = wk.shape[-1]
    return pl.pallas_call(
        _memkv_kernel,
        grid=(1,),
        in_specs=[pl.BlockSpec((n, d), lambda i: (0, 0)),
                  pl.BlockSpec((1, d), lambda i: (0, 0)),
                  _wspec((layer,), (d, w), lambda i: (0, 0)),
                  _wspec((layer,), (d, w), lambda i: (0, 0))],
        out_specs=[pl.BlockSpec((n, w), lambda i: (0, 0)), pl.BlockSpec((n, w), lambda i: (0, 0))],
        out_shape=[jax.ShapeDtypeStruct((n, w), F32), jax.ShapeDtypeStruct((n, w), F32)],
        compiler_params=_cparams(("arbitrary",)),
        name="memkv",
    )(mem, g, wk, wv)


def _cumsum_rows(g, c):
    row = lax.broadcasted_iota(jnp.int32, g.shape, 0)
    out = g
    sh = 1
    while sh < c:
        out = out + jnp.where(row >= sh, pltpu.roll(out, sh, 0), 0.0)
        sh *= 2
    return out


def _hgrn_chunk(q, g, v, st, c):
    k = 1.0 - jnp.exp(g)
    gc = _cumsum_rows(g, c)
    row = lax.broadcasted_iota(jnp.int32, (c, HEAD), 0)
    ra = lax.broadcasted_iota(jnp.int32, (c, c), 0)
    ca = lax.broadcasted_iota(jnp.int32, (c, c), 1)
    x = jnp.bitwise_xor(ra, ca)

    a = jnp.zeros((c, c), F32)
    b = c // 2
    while b >= 1:
        upper = jnp.bitwise_and(row, b) != 0
        if b >= 4:
            nb = c // (2 * b)
            gcb = gc.reshape(nb, 2 * b, HEAD)
            r = gcb[:, b - 1:b, :]
            up3 = upper.reshape(nb, 2 * b, HEAD)
            dlt = jnp.where(up3, gcb - r, r - gcb).reshape(c, HEAD)
        elif b == 2:
            t4 = jnp.bitwise_and(row, 3)
            g_next = pltpu.roll(g, c - 1, 0)
            g_prev = pltpu.roll(g, 1, 0)
            dlt = jnp.where(t4 == 0, g_next, jnp.where(t4 == 1, 0.0, jnp.where(t4 == 2, g, g + g_prev)))
        else:
            dlt = jnp.where(upper, g, 0.0)
        z = (jnp.where(upper, q, k) * jnp.exp(dlt)).astype(BF16)
        p = _dot_nt(z, z)
        a = p if b == c // 2 else jnp.where(x < 2 * b, p, a)
        b //= 2
    a = jnp.where(ra > ca, a, 0.0)

    vb = v.astype(BF16)
    o = _dot(a.astype(BF16), vb)
    o = o + jnp.sum(q * k, axis=-1, keepdims=True) * v
    o = o + _dot_nt((q * jnp.exp(gc)).astype(BF16), st.astype(BF16))
    g_end = gc[c - 1:c, :]
    khat = (k * jnp.exp(g_end - gc)).astype(BF16)
    st_new = st * jnp.exp(g_end) + lax.dot_general(vb, khat, (((0,), (0,)), ((), ())),
                                                   preferred_element_type=F32)
    return o, st_new


def _hgrn_out(o, gn, og):
    return _rms(o, gn) * og


def _hgrn_prompt_kernel(q_ref, g_ref, v_ref, og_ref, gn_ref, o_ref, s_ref, st_ref, *, c):
    h = pl.program_id(0)
    ci = pl.program_id(1)

    @pl.when(ci == 0)
    def _():
        st_ref[...] = jnp.zeros_like(st_ref)

    o, st_new = _hgrn_chunk(q_ref[...], g_ref[...], v_ref[...], st_ref[...], c)
    st_ref[...] = st_new
    o_ref[...] = _hgrn_out(o, gn_ref[pl.ds(h, 1), :], og_ref[...])

    @pl.when(ci == pl.num_programs(1) - 1)
    def _():
        s_ref[...] = st_new.T


def _hgrn_prompt(pack, gnorm, layer, seq, heads, c):
    m = pack.shape[0]
    blk = lambda off: pl.BlockSpec((c, HEAD), lambda h, i: (i, off + h))
    return pl.pallas_call(
        functools.partial(_hgrn_prompt_kernel, c=c),
        grid=(heads, seq // c),
        in_specs=[blk(0), blk(heads), blk(2 * heads), blk(3 * heads),
                  _wspec((layer,), (heads, HEAD), lambda h, i: (0, 0))],
        out_specs=[pl.BlockSpec((c, HEAD), lambda h, i: (i, h)),
                   pl.BlockSpec((None, HEAD, HEAD), lambda h, i: (h, 0, 0))],
        out_shape=[jax.ShapeDtypeStruct((m, heads * HEAD), F32),
                   jax.ShapeDtypeStruct((heads, HEAD, HEAD), F32)],
        scratch_shapes=[pltpu.VMEM((HEAD, HEAD), F32)],
        compiler_params=_cparams(("parallel", "arbitrary")),
        name="hgrn_prompt",
    )(pack, pack, pack, pack, gnorm)


def _hgrn_sample_kernel(q_ref, g_ref, v_ref, og_ref, gn_ref, s0_ref, obuf_ref, o_ref, s_ref, *, c, heads):
    del obuf_ref
    for h in range(heads):
        sl = slice(h * HEAD, (h + 1) * HEAD)
        o, st_new = _hgrn_chunk(q_ref[:, sl], g_ref[:, sl], v_ref[:, sl], s0_ref[h].T, c)
        o_ref[:, sl] = _hgrn_out(o, gn_ref[h:h + 1, :], og_ref[:, sl])
        s_ref[h] = st_new.T


def _hgrn_sample(pack, gnorm, s0, layer, o_buf, seq, batch, heads, c):
    w = heads * HEAD
    r0 = seq // c
    blk = lambda off: pl.BlockSpec((c, w), lambda b: (r0 + b, off))
    return pl.pallas_call(
        functools.partial(_hgrn_sample_kernel, c=c, heads=heads),
        grid=(batch,),
        in_specs=[blk(0), blk(1), blk(2), blk(3),
                  _wspec((layer,), (heads, HEAD), lambda b: (0, 0)),
                  _wspec((layer,), (None, heads, HEAD, HEAD), lambda b: (b, 0, 0, 0)),
                  pl.BlockSpec(memory_space=pl.ANY)],
        out_specs=[pl.BlockSpec((c, w), lambda b: (r0 + b, 0)),
                   pl.BlockSpec((None, heads, HEAD, HEAD), lambda b: (b, 0, 0, 0))],
        out_shape=[jax.ShapeDtypeStruct(o_buf.shape, F32),
                   jax.ShapeDtypeStruct(s0.shape[1:], F32)],
        input_output_aliases={6: 0},
        compiler_params=_cparams(("parallel",)),
        name="hgrn_sample",
    )(pack, pack, pack, pack, gnorm, s0, o_buf)


def _topk_mask(score, valid, axis):
    idx = lax.broadcasted_iota(jnp.int32, score.shape, axis)
    big = jnp.int32(2 ** 30)
    sg = jnp.where(valid, score, -jnp.inf)
    sel = jnp.zeros(score.shape, F32)
    for _ in range(MOBA_TOPK):
        mx = jnp.max(sg, axis=axis, keepdims=True)
        first = jnp.min(jnp.where(sg == mx, idx, big), axis=axis, keepdims=True)
        pick = jnp.logical_and(idx == first, mx > -jnp.inf)
        sel = jnp.where(pick, 1.0, sel)
        sg = jnp.where(pick, -jnp.inf, sg)
    return sel


def _moba_prompt_kernel(it_ref, jt_ref, last_ref, q_ref, k_ref, vt_ref, o_ref,
                        qs_ref, m_ref, l_ref, acc_ref, sel_ref, km_ref, *, heads, nbp):
    s = pl.program_id(0)
    i = it_ref[s]
    j = jt_ref[s]
    blk = MOBA_BLOCK
    scale = HEAD ** -0.5

    @pl.when(s == 0)
    def _():
        km_ref[...] = jnp.zeros_like(km_ref)

    k = k_ref[...]

    @pl.when(j == i)
    def _():
        km_ref[pl.ds(i, 1), :] = jnp.mean(k, axis=0, keepdims=True)
        qt = q_ref[...].T
        qs_ref[...] = (qt * scale).astype(BF16)
        key = lax.broadcasted_iota(jnp.int32, (blk, blk), 0)
        qry = lax.broadcasted_iota(jnp.int32, (blk, blk), 1)
        causal = jnp.where(key <= qry, 0.0, NEG)
        bidx = lax.broadcasted_iota(jnp.int32, (nbp, blk), 0)
        for h in range(heads):
            sl = slice(h * HEAD, (h + 1) * HEAD)
            gate = _dot3(km_ref[:, sl], qt[sl, :])
            sel_ref[h] = _topk_mask(gate, bidx < i, 0)
            st = _dot(k[:, sl].astype(BF16), qs_ref[sl, :]) + causal
            m = jnp.max(st, axis=0, keepdims=True)
            p = jnp.exp(st - m)
            m_ref[h:h + 1, :] = m
            l_ref[h:h + 1, :] = jnp.sum(p, axis=0, keepdims=True)
            acc_ref[sl, :] = _dot(vt_ref[sl, :], p.astype(BF16))

    @pl.when(j != i)
    def _():
        for h in range(heads):
            sl = slice(h * HEAD, (h + 1) * HEAD)
            bias = (sel_ref[h, pl.ds(j, 1), :] - 1.0) * (-NEG)
            st = _dot(k[:, sl].astype(BF16), qs_ref[sl, :]) + bias
            m_old = m_ref[h:h + 1, :]
            m_new = jnp.maximum(m_old, jnp.max(st, axis=0, keepdims=True))
            alpha = jnp.exp(m_old - m_new)
            p = jnp.exp(st - m_new)
            m_ref[h:h + 1, :] = m_new
            l_ref[h:h + 1, :] = alpha * l_ref[h:h + 1, :] + jnp.sum(p, axis=0, keepdims=True)
            acc_ref[sl, :] = alpha * acc_ref[sl, :] + _dot(vt_ref[sl, :], p.astype(BF16))

    @pl.when(last_ref[s] == 1)
    def _():
        for h in range(heads):
            sl = slice(h * HEAD, (h + 1) * HEAD)
            o_ref[:, sl] = (acc_ref[sl, :] / l_ref[h:h + 1, :]).T


def _moba_prompt(qk, vt, seq, heads):
    m = qk.shape[0]
    w = heads * HEAD
    blk = MOBA_BLOCK
    nb = seq // blk
    nbp = -(-nb // 8) * 8
    it, jt, last = [], [], []
    for i in range(nb):
        for j in [i] + list(range(i)):
            it.append(i)
            jt.append(j)
            last.append(0)
        last[-1] = 1
    tabs = [jnp.asarray(np.asarray(t, np.int32)) for t in (it, jt, last)]
    grid_spec = pltpu.PrefetchScalarGridSpec(
        num_scalar_prefetch=3,
        grid=(len(it),),
        in_specs=[pl.BlockSpec((blk, w), lambda s, it, jt, la: (it[s], 0)),
                  pl.BlockSpec((blk, w), lambda s, it, jt, la: (jt[s], 1)),
                  pl.BlockSpec((w, blk), lambda s, it, jt, la: (0, jt[s]))],
        out_specs=pl.BlockSpec((blk, w), lambda s, it, jt, la: (it[s], 0)),
        scratch_shapes=[pltpu.VMEM((w, blk), BF16),
                        pltpu.VMEM((heads, blk), F32),
                        pltpu.VMEM((heads, blk), F32),
                        pltpu.VMEM((w, blk), F32),
                        pltpu.VMEM((heads, nbp, blk), F32),
                        pltpu.VMEM((nbp, w), F32)],
    )
    return pl.pallas_call(
        functools.partial(_moba_prompt_kernel, heads=heads, nbp=nbp),
        grid_spec=grid_spec,
        out_shape=jax.ShapeDtypeStruct((m, w), F32),
        compiler_params=_cparams(("arbitrary",)),
        name="moba_prompt",
    )(*tabs, qk, qk, vt)


def _moba_sample_kernel(pt_ref, *refs, heads, t_new, pps, nb, nbp, page):
    del pt_ref
    k_refs = refs[:pps]
    v_refs = refs[pps:2 * pps]
    q_ref, kn_ref, vn_ref, obuf_ref, o_ref, m_ref, l_ref, ob_ref, km_ref = refs[2 * pps:]
    del obuf_ref
    step = pl.program_id(1)
    ppb = MOBA_BLOCK // page
    bps = pps // ppb
    scale = HEAD ** -0.5

    q = q_ref[...]
    qs = (q * scale).astype(BF16)

    @pl.when(step == 0)
    def _():
        km_ref[...] = jnp.zeros_like(km_ref)

    def partial_block(qh, ks, vs, bias):
        sc = [_dot_nt(qh, kk.astype(BF16)) for kk in ks]
        if bias is not None:
            sc = [x + bias for x in sc]
        m = sc[0].max(axis=-1, keepdims=True)
        for x in sc[1:]:
            m = jnp.maximum(m, x.max(axis=-1, keepdims=True))
        ps = [jnp.exp(x - m) for x in sc]
        l = ps[0].sum(axis=-1, keepdims=True)
        for p in ps[1:]:
            l = l + p.sum(axis=-1, keepdims=True)
        o = _dot(ps[0].astype(BF16), vs[0].astype(BF16))
        for p, vv in zip(ps[1:], vs[1:]):
            o = o + _dot(p.astype(BF16), vv.astype(BF16))
        return m, l, o

    for bi in range(bps):
        jb = step * bps + bi
        pages = range(bi * ppb, (bi + 1) * ppb)
        ksum = k_refs[pages[0]][...].sum(axis=0)
        for p in pages[1:]:
            ksum = ksum + k_refs[p][...].sum(axis=0)
        km_ref[jb] = ksum * (1.0 / MOBA_BLOCK)
        for h in range(heads):
            sl = slice(h * HEAD, (h + 1) * HEAD)
            m, l, o = partial_block(qs[:, sl], [k_refs[p][:, h, :] for p in pages],
                                    [v_refs[p][:, h, :] for p in pages], None)
            m_ref[jb, h] = jnp.broadcast_to(m, (t_new, LANES))
            l_ref[jb, h] = jnp.broadcast_to(l, (t_new, LANES))
            ob_ref[jb, h] = o

    @pl.when(step == pl.num_programs(1) - 1)
    def _():
        tq = lax.broadcasted_iota(jnp.int32, (t_new, LANES), 0)
        kk = lax.broadcasted_iota(jnp.int32, (t_new, LANES), 1)
        causal = jnp.where(kk <= tq, 0.0, NEG)
        zpad = jnp.zeros((LANES - t_new, HEAD), F32)
        bidx = lax.broadcasted_iota(jnp.int32, (t_new, nbp), 1)
        for h in range(heads):
            sl = slice(h * HEAD, (h + 1) * HEAD)
            k_own = jnp.concatenate([kn_ref[:, sl], zpad], axis=0)
            v_own = jnp.concatenate([vn_ref[:, sl], zpad], axis=0)
            m_o, l_o, o_o = partial_block(qs[:, sl], [k_own], [v_own], causal)
            gate = _dot3_nt(q[:, sl], km_ref[:, h, :])
            sel = _topk_mask(gate, bidx < nb, 1)
            m_all = m_o
            for jb in range(nb):
                m_all = jnp.maximum(m_all, jnp.where(sel[:, jb:jb + 1] > 0.5, m_ref[jb, h][:, 0:1], NEG))
            w_o = jnp.exp(m_o - m_all)
            l_tot = w_o * l_o
            o_tot = w_o * o_o
            for jb in range(nb):
                picked = sel[:, jb:jb + 1] > 0.5
                wj = jnp.where(picked, jnp.exp(jnp.where(picked, m_ref[jb, h][:, 0:1], NEG) - m_all), 0.0)
                l_tot = l_tot + wj * l_ref[jb, h][:, 0:1]
                o_tot = o_tot + wj * ob_ref[jb, h]
            o_ref[:, sl] = o_tot / l_tot


def _moba_sample(cache_k, cache_v, layer, page_table, qk, v, o_buf, seq, t_new, pps):
    depth, n_pool, page, heads, _ = cache_k.shape
    batch, n_pages = page_table.shape
    w = heads * HEAD
    nb = n_pages * page // MOBA_BLOCK
    nbp = -(-nb // LANES) * LANES
    n_steps = n_pages // pps
    r0 = seq // t_new

    def page_spec(p):
        return pl.BlockSpec((None, None, page, heads, HEAD),
                            lambda b, t, pt: (layer, pt[b * n_pages + t * pps + p], 0, 0, 0))

    grid_spec = pltpu.PrefetchScalarGridSpec(
        num_scalar_prefetch=1,
        grid=(batch, n_steps),
        in_specs=([page_spec(p) for p in range(pps)] + [page_spec(p) for p in range(pps)] +
                  [pl.BlockSpec((t_new, w), lambda b, t, pt: (r0 + b, 0)),
                   pl.BlockSpec((t_new, w), lambda b, t, pt: (r0 + b, 1)),
                   pl.BlockSpec((t_new, w), lambda b, t, pt: (r0 + b, 0)),
                   pl.BlockSpec(memory_space=pl.ANY)]),
        out_specs=pl.BlockSpec((t_new, w), lambda b, t, pt: (r0 + b, 0)),
        scratch_shapes=[pltpu.VMEM((nb, heads, t_new, LANES), F32),
                        pltpu.VMEM((nb, heads, t_new, LANES), F32),
                        pltpu.VMEM((nb, heads, t_new, HEAD), F32),
                        pltpu.VMEM((nbp, heads, HEAD), F32)],
    )
    n_in = 1 + 2 * pps + 4
    return pl.pallas_call(
        functools.partial(_moba_sample_kernel, heads=heads, t_new=t_new, pps=pps, nb=nb, nbp=nbp, page=page),
        grid_spec=grid_spec,
        out_shape=jax.ShapeDtypeStruct(o_buf.shape, F32),
        input_output_aliases={n_in - 1: 0},
        compiler_params=_cparams(("parallel", "arbitrary")),
        name="moba_sample",
    )(page_table.reshape(-1), *([cache_k] * pps), *([cache_v] * pps), qk, qk, v, o_buf)


def _merge_kernel(or_ref, oa_ref, gr_ref, ga_ref, wr_ref, wa_ref, o_ref):
    yr = _dot(or_ref[...].astype(BF16), wr_ref[...])
    ya = _dot(oa_ref[...].astype(BF16), wa_ref[...])
    o_ref[...] = (gr_ref[...].astype(F32) * yr + ga_ref[...].astype(F32) * ya).astype(BF16)


def _merge(o_r, o_a, gates, w_r, w_a, layer, bm):
    m, k = o_r.shape
    d = w_r.shape[-1]
    return pl.pallas_call(
        _merge_kernel,
        grid=(m // bm,),
        in_specs=[pl.BlockSpec((bm, k), lambda i: (i, 0)),
                  pl.BlockSpec((bm, k), lambda i: (i, 0)),
                  pl.BlockSpec((bm, d), lambda i: (i, 0)),
                  pl.BlockSpec((bm, d), lambda i: (i, 1)),
                  _wspec((layer,), (k, d), lambda i: (0, 0), resident=True),
                  _wspec((layer,), (k, d), lambda i: (0, 0), resident=True)],
        out_specs=pl.BlockSpec((bm, d), lambda i: (i, 0)),
        out_shape=jax.ShapeDtypeStruct((m, d), BF16),
        compiler_params=_cparams(("parallel",)),
        name="merge",
    )(o_r, o_a, gates, gates, w_r, w_a)


def _cross_head(q, k, v):
    s = _dot_nt(q.astype(BF16), k.astype(BF16))
    p = jnp.exp(s - jnp.max(s, axis=-1, keepdims=True))
    l = jnp.sum(p, axis=-1, keepdims=True)
    return _dot(p.astype(BF16), v.astype(BF16)) / l


def _cross_prompt_kernel(q_ref, k_ref, v_ref, o_ref, *, heads):
    for h in range(heads):
        sl = slice(h * HEAD, (h + 1) * HEAD)
        o_ref[:, sl] = _cross_head(q_ref[:, sl], k_ref[:, sl], v_ref[:, sl])


def _cross_sample_kernel(q_ref, k_ref, v_ref, obuf_ref, o_ref, *, heads):
    del obuf_ref
    for h in range(heads):
        sl = slice(h * HEAD, (h + 1) * HEAD)
        o_ref[:, sl] = _cross_head(q_ref[:, sl], k_ref[:, h, :], v_ref[:, h, :])


def _cross_prompt(qc, mk, mv, seq, heads, bq):
    m, w = qc.shape
    n = mk.shape[0]
    return pl.pallas_call(
        functools.partial(_cross_prompt_kernel, heads=heads),
        grid=(seq // bq,),
        in_specs=[pl.BlockSpec((bq, w), lambda i: (i, 0)),
                  pl.BlockSpec((n, w), lambda i: (0, 0)),
                  pl.BlockSpec((n, w), lambda i: (0, 0))],
        out_specs=pl.BlockSpec((bq, w), lambda i: (i, 0)),
        out_shape=jax.ShapeDtypeStruct((m, w), F32),
        compiler_params=_cparams(("parallel",)),
        name="cross_prompt",
    )(qc, mk, mv)


def _cross_sample(qc, mem_k, mem_v, layer, o_buf, seq, t_new):
    depth, batch, n, heads, _ = mem_k.shape
    w = heads * HEAD
    r0 = seq // t_new
    mem_spec = pl.BlockSpec((None, None, n, heads, HEAD), lambda b: (layer, b, 0, 0, 0))
    return pl.pallas_call(
        functools.partial(_cross_sample_kernel, heads=heads),
        grid=(batch,),
        in_specs=[pl.BlockSpec((t_new, w), lambda b: (r0 + b, 0)), mem_spec, mem_spec,
                  pl.BlockSpec(memory_space=pl.ANY)],
        out_specs=pl.BlockSpec((t_new, w), lambda b: (r0 + b, 0)),
        out_shape=jax.ShapeDtypeStruct(o_buf.shape, F32),
        input_output_aliases={3: 0},
        compiler_params=_cparams(("parallel",)),
        name="cross_sample",
    )(qc, mem_k, mem_v, o_buf)


def _rope_tables(seq, past_len, batch, t_new):
    half = HEAD // 2
    inv_freq = ROPE_THETA ** (-jnp.arange(half, dtype=F32) / half)
    pos = jnp.concatenate([jnp.arange(seq, dtype=jnp.int32),
                           jnp.tile(past_len + jnp.arange(t_new, dtype=jnp.int32), batch)])
    ang = pos.astype(F32)[:, None] * inv_freq[None, :]
    cos = jnp.cos(ang)
    sin = jnp.sin(ang)
    return jnp.concatenate([cos, cos], axis=-1), jnp.concatenate([-sin, sin], axis=-1)


def kernel(x_prompt, x_sample, cache_k, cache_v, state_hgrn, cache_mem_k, cache_mem_v, page_table, mem_prompt,
           norm_pre, norm_post, ffn_gate, ffn_up, ffn_down, w_in, lower_bounds, hgrn_gnorm,
           w_branch_r, w_branch_a, w_out, mem_norm, w_cq, w_ck, w_cv, w_co):
    b_p, seq, d = x_prompt.shape
    batch, t_new, _ = x_sample.shape
    depth, n_pool, page, h_a, _ = cache_k.shape
    h_r = state_hgrn.shape[2]
    h_m = cache_mem_k.shape[3]
    n_mem = cache_mem_k.shape[2]
    n_pages = page_table.shape[1]
    past_len = n_pages * page
    assert b_p == 1 and h_r == 8 and h_a == 8
    assert seq % MOBA_BLOCK == 0 and past_len % MOBA_BLOCK == 0 and t_new <= MOBA_BLOCK
    assert t_new % 8 == 0 and seq % t_new == 0 and MOBA_BLOCK % page == 0
    m = seq + batch * t_new
    w8 = 8 * HEAD

    bm = _pick(m, (768, 512, 384, 256, 128))
    bm_res = _pick(m, (256, 128))
    bn_ff = _pick(ffn_gate.shape[-1], (512, 256))
    hg_c = _pick(seq, (256,))
    pps = _pick(n_pages, (8, 4, 2))

    cos_t, sin_t = _rope_tables(seq, past_len, batch, t_new)
    lower_bounds = lower_bounds.astype(F32)
    bf = lambda a: a.astype(BF16)
    ffn_gate, ffn_up, ffn_down, w_in = bf(ffn_gate), bf(ffn_up), bf(ffn_down), bf(w_in)
    w_branch_r, w_branch_a, w_out = bf(w_branch_r), bf(w_branch_a), bf(w_out)
    w_cq, w_ck, w_cv, w_co = bf(w_cq), bf(w_ck), bf(w_cv), bf(w_co)

    x = jnp.concatenate([x_prompt.reshape(seq, d), x_sample.reshape(batch * t_new, d)], axis=0)
    u = _norm_cast(x, norm_pre[0, 0][None], bm)

    outs = {k: [] for k in ("kp", "vp", "sp", "mkp", "mvp", "ks", "vs", "ss")}
    for l in range(depth):
        g_pre = lambda i: norm_pre[l, i][None]
        g_post = lambda i: norm_post[l, i][None]
        g_next_layer = norm_pre[l + 1, 0][None] if l + 1 < depth else jnp.ones((1, d), F32)

        mk_p, mv_p = _memkv(mem_prompt.reshape(n_mem, d), mem_norm[l][None], w_ck, w_cv, l)

        h = _ffn_up(u, ffn_gate, ffn_up, (l, 0), bm, bn_ff)
        x, u = _proj_res(h, ffn_down, (l, 0), x, g_post(0), g_pre(1), 0.5, bm_res)

        pack = _win_hgrn(u, w_in, lower_bounds, l, bm)
        qk = _win_qk(u, w_in, l, cos_t, sin_t, 4, h_a, bm)
        v, vt = _win_v(u, w_in, l, 6, h_a, bm)
        gates = _win_gates(u, w_in, l, 7, 2 * d // w8, bm)

        o_r, s_p = _hgrn_prompt(pack, hgrn_gnorm, l, seq, h_r, hg_c)
        o_r, s_s = _hgrn_sample(pack, hgrn_gnorm, state_hgrn, l, o_r, seq, batch, h_r, t_new)

        o_a = _moba_prompt(qk, vt, seq, h_a)
        o_a = _moba_sample(cache_k, cache_v, l, page_table, qk, v, o_a, seq, t_new, pps)

        merged = _merge(o_r, o_a, gates, w_branch_r, w_branch_a, l, bm_res)
        x, u = _proj_res(merged, w_out, (l,), x, g_post(1), g_pre(2), 1.0, bm_res)

        qc = _scaled_proj(u, w_cq, (l,), HEAD ** -0.5, bm)
        oc = _cross_prompt(qc, mk_p, mv_p, seq, h_m, _pick(seq, (512, 256)))
        oc = _cross_sample(qc, cache_mem_k, cache_mem_v, l, oc, seq, t_new)
        x, u = _proj_res(oc, w_co, (l,), x, g_post(2), g_pre(3), 1.0, bm_res)

        h = _ffn_up(u, ffn_gate, ffn_up, (l, 1), bm, bn_ff)
        x, u = _proj_res(h, ffn_down, (l, 1), x, g_post(3), g_next_layer, 0.5, bm_res)

        k_all = qk[:, w8:]
        outs["kp"].append(k_all[:seq].reshape(1, seq, h_a, HEAD))
        outs["vp"].append(v[:seq].reshape(1, seq, h_a, HEAD))
        outs["sp"].append(s_p.reshape(1, h_r, HEAD, HEAD))
        outs["mkp"].append(mk_p.reshape(1, n_mem, h_m, HEAD))
        outs["mvp"].append(mv_p.reshape(1, n_mem, h_m, HEAD))
        outs["ks"].append(k_all[seq:].reshape(batch, t_new, h_a, HEAD))
        outs["vs"].append(v[seq:].reshape(batch, t_new, h_a, HEAD))
        outs["ss"].append(s_s)

    st = lambda k: jnp.stack(outs[k])
    return (x[:seq].reshape(1, seq, d), x[seq:].reshape(batch, t_new, d),
            st("kp"), st("vp"), st("sp"), st("mkp"), st("mvp"), st("ks"), st("vs"), st("ss"))
```

```python
import functools

import numpy as np
import jax
import jax.numpy as jnp
from jax import lax
from jax.experimental import pallas as pl
from jax.experimental.pallas import tpu as pltpu

F32 = jnp.float32
BF16 = jnp.bfloat16

MOBA_BLOCK = 256
MOBA_TOPK = 3
ROPE_THETA = 10000.0
EPS = 1e-6
HEAD = 128
LANES = 128
NEG = -1e30
LOG2E = 1.4426950408889634
VMEM_LIMIT = 56 * 1024 * 1024


def _cparams(sem, vmem=VMEM_LIMIT):
    return pltpu.CompilerParams(dimension_semantics=sem, vmem_limit_bytes=vmem)


def _pick(n, cands):
    for c in cands:
        if n % c == 0:
            return c
    raise ValueError(f"no tile in {cands} divides {n}")


def _wspec(lead, shape, index_map, resident=False):
    full_shape = (None,) * len(lead) + tuple(shape)
    full_map = lambda *a: tuple(lead) + tuple(index_map(*a))
    if resident:
        return pl.BlockSpec(full_shape, full_map, pipeline_mode=pl.Buffered(1))
    return pl.BlockSpec(full_shape, full_map)


def _dot(a, b):
    return jnp.dot(a, b, preferred_element_type=F32)


def _dot_nt(a, b):
    return lax.dot_general(a, b, (((1,), (1,)), ((), ())), preferred_element_type=F32)


def _split2(a):
    hi = a.astype(BF16)
    lo = (a - hi.astype(F32)).astype(BF16)
    return hi, lo


def _dot3_nt(a, b):
    ah, al = _split2(a)
    bh, bl = _split2(b)
    return _dot_nt(ah, bh) + _dot_nt(ah, bl) + _dot_nt(al, bh)


def _dot3(a, b):
    ah, al = _split2(a)
    bh, bl = _split2(b)
    return _dot(ah, bh) + _dot(ah, bl) + _dot(al, bh)


def _sigmoid(x):
    return 1.0 / (1.0 + jnp.exp(-x))


def _silu(x):
    return x * _sigmoid(x)


def _rms(y, g):
    ms = jnp.mean(y * y, axis=-1, keepdims=True)
    return y * lax.rsqrt(ms + EPS) * g


def _norm_kernel(x_ref, g_ref, u_ref):
    u_ref[...] = _rms(x_ref[...], g_ref[...]).astype(BF16)


def _norm_cast(x, g, bm):
    m, d = x.shape
    return pl.pallas_call(
        _norm_kernel,
        grid=(m // bm,),
        in_specs=[pl.BlockSpec((bm, d), lambda i: (i, 0)), pl.BlockSpec((1, d), lambda i: (0, 0))],
        out_specs=pl.BlockSpec((bm, d), lambda i: (i, 0)),
        out_shape=jax.ShapeDtypeStruct((m, d), BF16),
        compiler_params=_cparams(("parallel",)),
        name="norm_cast",
    )(x, g)


def _ffn_up_kernel(u_ref, wg_ref, wu_ref, h_ref):
    u = u_ref[...]
    h_ref[...] = (_silu(_dot(u, wg_ref[...])) * _dot(u, wu_ref[...])).astype(BF16)


def _ffn_up(u, wg, wu, lead, bm, bn):
    m, d = u.shape
    f = wg.shape[-1]
    return pl.pallas_call(
        _ffn_up_kernel,
        grid=(f // bn, m // bm),
        in_specs=[pl.BlockSpec((bm, d), lambda n, i: (i, 0)),
                  _wspec(lead, (d, bn), lambda n, i: (0, n)),
                  _wspec(lead, (d, bn), lambda n, i: (0, n))],
        out_specs=pl.BlockSpec((bm, bn), lambda n, i: (i, n)),
        out_shape=jax.ShapeDtypeStruct((m, f), BF16),
        compiler_params=_cparams(("parallel", "parallel")),
        name="ffn_up",
    )(u, wg, wu)


def _proj_res_kernel(a_ref, w_ref, x_ref, gp_ref, gn_ref, xo_ref, uo_ref, *, scale):
    y = _dot(a_ref[...].astype(BF16), w_ref[...])
    xn = x_ref[...] + scale * _rms(y, gp_ref[...])
    xo_ref[...] = xn
    uo_ref[...] = _rms(xn, gn_ref[...]).astype(BF16)


def _proj_res(a, w, lead, x, g_post, g_next, scale, bm):
    m, k = a.shape
    d = w.shape[-1]
    return pl.pallas_call(
        functools.partial(_proj_res_kernel, scale=scale),
        grid=(m // bm,),
        in_specs=[pl.BlockSpec((bm, k), lambda i: (i, 0)),
                  _wspec(lead, (k, d), lambda i: (0, 0), resident=True),
                  pl.BlockSpec((bm, d), lambda i: (i, 0)),
                  pl.BlockSpec((1, d), lambda i: (0, 0)),
                  pl.BlockSpec((1, d), lambda i: (0, 0))],
        out_specs=[pl.BlockSpec((bm, d), lambda i: (i, 0)), pl.BlockSpec((bm, d), lambda i: (i, 0))],
        out_shape=[jax.ShapeDtypeStruct((m, d), F32), jax.ShapeDtypeStruct((m, d), BF16)],
        compiler_params=_cparams(("parallel",)),
        name="proj_res",
    )(a, w, x, g_post, g_next)


def _win_hgrn_kernel(u_ref, w_ref, lb_ref, o_ref, *, layer):
    n = pl.program_id(0)
    acc = _dot(u_ref[...], w_ref[...])

    @pl.when(n == 0)
    def _():
        o_ref[...] = _silu(acc) * (HEAD ** -0.5)

    @pl.when(n == 1)
    def _():
        lbs = lb_ref[...]
        e = jnp.exp(lbs - jnp.max(lbs, axis=0, keepdims=True))
        sm = e / jnp.sum(e, axis=0, keepdims=True)
        cum = sm[0:1, :]
        for i in range(1, layer + 1):
            cum = cum + sm[i:i + 1, :]
        lb = cum - sm[0:1, :]
        ls = jnp.minimum(acc, 0.0) - jnp.log1p(jnp.exp(-jnp.abs(acc)))
        a = jnp.log(lb)
        b = jnp.log1p(-lb) + ls
        o_ref[...] = jnp.maximum(a, b) + jnp.log1p(jnp.exp(-jnp.abs(a - b)))

    @pl.when(n == 2)
    def _():
        o_ref[...] = acc

    @pl.when(n == 3)
    def _():
        o_ref[...] = _silu(acc)


def _win_hgrn(u, w_in, lower_bounds, layer, bm):
    m, d = u.shape
    w = 8 * HEAD
    depth = lower_bounds.shape[0]
    return pl.pallas_call(
        functools.partial(_win_hgrn_kernel, layer=layer),
        grid=(4, m // bm),
        in_specs=[pl.BlockSpec((bm, d), lambda n, i: (i, 0)),
                  _wspec((layer,), (d, w), lambda n, i: (0, n)),
                  pl.BlockSpec((depth, w), lambda n, i: (0, 0))],
        out_specs=pl.BlockSpec((bm, w), lambda n, i: (i, n)),
        out_shape=jax.ShapeDtypeStruct((m, 4 * w), F32),
        compiler_params=_cparams(("parallel", "parallel")),
        name="win_hgrn",
    )(u, w_in, lower_bounds)


def _win_qk_kernel(u_ref, w_ref, cos_ref, sin_ref, o_ref, *, heads):
    acc = _dot(u_ref[...], w_ref[...])
    c = cos_ref[...]
    s = sin_ref[...]
    for h in range(heads):
        xh = acc[:, h * HEAD:(h + 1) * HEAD]
        o_ref[:, h * HEAD:(h + 1) * HEAD] = xh * c + pltpu.roll(xh, HEAD // 2, 1) * s


def _win_qk(u, w_in, layer, cos_t, sin_t, col_blk, heads, bm):
    m, d = u.shape
    w = heads * HEAD
    return pl.pallas_call(
        functools.partial(_win_qk_kernel, heads=heads),
        grid=(2, m // bm),
        in_specs=[pl.BlockSpec((bm, d), lambda n, i: (i, 0)),
                  _wspec((layer,), (d, w), lambda n, i: (0, col_blk + n)),
                  pl.BlockSpec((bm, HEAD), lambda n, i: (i, 0)),
                  pl.BlockSpec((bm, HEAD), lambda n, i: (i, 0))],
        out_specs=pl.BlockSpec((bm, w), lambda n, i: (i, n)),
        out_shape=jax.ShapeDtypeStruct((m, 2 * w), F32),
        compiler_params=_cparams(("parallel", "parallel")),
        name="win_qk",
    )(u, w_in, cos_t, sin_t)


def _win_v_kernel(u_ref, w_ref, v_ref, vt_ref):
    acc = _dot(u_ref[...], w_ref[...])
    v_ref[...] = acc
    vt_ref[...] = acc.T.astype(BF16)


def _win_v(u, w_in, layer, col_blk, heads, bm):
    m, d = u.shape
    w = heads * HEAD
    return pl.pallas_call(
        _win_v_kernel,
        grid=(m // bm,),
        in_specs=[pl.BlockSpec((bm, d), lambda i: (i, 0)),
                  _wspec((layer,), (d, w), lambda i: (0, col_blk))],
        out_specs=[pl.BlockSpec((bm, w), lambda i: (i, 0)), pl.BlockSpec((w, bm), lambda i: (0, i))],
        out_shape=[jax.ShapeDtypeStruct((m, w), F32), jax.ShapeDtypeStruct((w, m), BF16)],
        compiler_params=_cparams(("parallel",)),
        name="win_v",
    )(u, w_in)


def _win_gate_kernel(u_ref, w_ref, o_ref):
    o_ref[...] = _sigmoid(_dot(u_ref[...], w_ref[...])).astype(BF16)


def _win_gates(u, w_in, layer, col_blk, n_blk, bm):
    m, d = u.shape
    w = 8 * HEAD
    return pl.pallas_call(
        _win_gate_kernel,
        grid=(n_blk, m // bm),
        in_specs=[pl.BlockSpec((bm, d), lambda n, i: (i, 0)),
                  _wspec((layer,), (d, w), lambda n, i: (0, col_blk + n))],
        out_specs=pl.BlockSpec((bm, w), lambda n, i: (i, n)),
        out_shape=jax.ShapeDtypeStruct((m, n_blk * w), BF16),
        compiler_params=_cparams(("parallel", "parallel")),
        name="win_gates",
    )(u, w_in)


def _scaled_proj_kernel(u_ref, w_ref, o_ref, *, scale):
    o_ref[...] = _dot(u_ref[...], w_ref[...]) * scale


def _scaled_proj(u, w, lead, scale, bm):
    m, d = u.shape
    n = w.shape[-1]
    return pl.pallas_call(
        functools.partial(_scaled_proj_kernel, scale=scale),
        grid=(m // bm,),
        in_specs=[pl.BlockSpec((bm, d), lambda i: (i, 0)),
                  _wspec(lead, (d, n), lambda i: (0, 0), resident=True)],
        out_specs=pl.BlockSpec((bm, n), lambda i: (i, 0)),
        out_shape=jax.ShapeDtypeStruct((m, n), F32),
        compiler_params=_cparams(("parallel",)),
        name="scaled_proj",
    )(u, w)


def _memkv_kernel(mem_ref, g_ref, wk_ref, wv_ref, k_ref, v_ref):
    mn = _rms(mem_ref[...], g_ref[...]).astype(BF16)
    k_ref[...] = _dot(mn, wk_ref[...])
    v_ref[...] = _dot(mn, wv_ref[...])


def _memkv(mem, g, wk, wv, layer):
    n, d = mem.shape
    w = wk.shape[-1]
    return pl.pallas_call(
        _memkv_kernel,
        grid=(1,),
        in_specs=[pl.BlockSpec((n, d), lambda i: (0, 0)),
                  pl.BlockSpec((1, d), lambda i: (0, 0)),
                  _wspec((layer,), (d, w), lambda i: (0, 0)),
                  _wspec((layer,), (d, w), lambda i: (0, 0))],
        out_specs=[pl.BlockSpec((n, w), lambda i: (0, 0)), pl.BlockSpec((n, w), lambda i: (0, 0))],
        out_shape=[jax.ShapeDtypeStruct((n, w), F32), jax.ShapeDtypeStruct((n, w), F32)],
        compiler_params=_cparams(("arbitrary",)),
        name="memkv",
    )(mem, g, wk, wv)


def _cumsum_rows(g, c):
    row = lax.broadcasted_iota(jnp.int32, g.shape, 0)
    out = g
    sh = 1
    while sh < c:
        out = out + jnp.where(row >= sh, pltpu.roll(out, sh, 0), 0.0)
        sh *= 2
    return out


def _hgrn_chunk(q, g, v, st, c):
    k = 1.0 - jnp.exp(g)
    gc = _cumsum_rows(g, c)
    row = lax.broadcasted_iota(jnp.int32, (c, HEAD), 0)
    ra = lax.broadcasted_iota(jnp.int32, (c, c), 0)
    ca = lax.broadcasted_iota(jnp.int32, (c, c), 1)
    x = jnp.bitwise_xor(ra, ca)

    a = jnp.zeros((c, c), F32)
    b = c // 2
    while b >= 1:
        upper = jnp.bitwise_and(row, b) != 0
        if b >= 4:
            nb = c // (2 * b)
            gcb = gc.reshape(nb, 2 * b, HEAD)
            r = gcb[:, b - 1:b, :]
            up3 = upper.reshape(nb, 2 * b, HEAD)
            dlt = jnp.where(up3, gcb - r, r - gcb).reshape(c, HEAD)
        elif b == 2:
            t4 = jnp.bitwise_and(row, 3)
            g_next = pltpu.roll(g, c - 1, 0)
            g_prev = pltpu.roll(g, 1, 0)
            dlt = jnp.where(t4 == 0, g_next, jnp.where(t4 == 1, 0.0, jnp.where(t4 == 2, g, g + g_prev)))
        else:
            dlt = jnp.where(upper, g, 0.0)
        z = (jnp.where(upper, q, k) * jnp.exp(dlt)).astype(BF16)
        p = _dot_nt(z, z)
        a = p if b == c // 2 else jnp.where(x < 2 * b, p, a)
        b //= 2
    a = jnp.where(ra > ca, a, 0.0)

    vb = v.astype(BF16)
    o = _dot(a.astype(BF16), vb)
    o = o + jnp.sum(q * k, axis=-1, keepdims=True) * v
    o = o + _dot_nt((q * jnp.exp(gc)).astype(BF16), st.astype(BF16))
    g_end = gc[c - 1:c, :]
    khat = (k * jnp.exp(g_end - gc)).astype(BF16)
    st_new = st * jnp.exp(g_end) + lax.dot_general(vb, khat, (((0,), (0,)), ((), ())),
                                                   preferred_element_type=F32)
    return o, st_new


def _hgrn_out(o, gn, og):
    return _rms(o, gn) * og


def _hgrn_prompt_kernel(q_ref, g_ref, v_ref, og_ref, gn_ref, o_ref, s_ref, st_ref, *, c):
    h = pl.program_id(0)
    ci = pl.program_id(1)

    @pl.when(ci == 0)
    def _():
        st_ref[...] = jnp.zeros_like(st_ref)

    o, st_new = _hgrn_chunk(q_ref[...], g_ref[...], v_ref[...], st_ref[...], c)
    st_ref[...] = st_new
    o_ref[...] = _hgrn_out(o, gn_ref[pl.ds(h, 1), :], og_ref[...])

    @pl.when(ci == pl.num_programs(1) - 1)
    def _():
        s_ref[...] = st_new.T


def _hgrn_prompt(pack, gnorm, layer, seq, heads, c):
    m = pack.shape[0]
    blk = lambda off: pl.BlockSpec((c, HEAD), lambda h, i: (i, off + h))
    return pl.pallas_call(
        functools.partial(_hgrn_prompt_kernel, c=c),
        grid=(heads, seq // c),
        in_specs=[blk(0), blk(heads), blk(2 * heads), blk(3 * heads),
                  _wspec((layer,), (heads, HEAD), lambda h, i: (0, 0))],
        out_specs=[pl.BlockSpec((c, HEAD), lambda h, i: (i, h)),
                   pl.BlockSpec((None, HEAD, HEAD), lambda h, i: (h, 0, 0))],
        out_shape=[jax.ShapeDtypeStruct((m, heads * HEAD), F32),
                   jax.ShapeDtypeStruct((heads, HEAD, HEAD), F32)],
        scratch_shapes=[pltpu.VMEM((HEAD, HEAD), F32)],
        compiler_params=_cparams(("parallel", "arbitrary")),
        name="hgrn_prompt",
    )(pack, pack, pack, pack, gnorm)


def _hgrn_sample_kernel(q_ref, g_ref, v_ref, og_ref, gn_ref, s0_ref, obuf_ref, o_ref, s_ref, *, c, heads):
    del obuf_ref
    for h in range(heads):
        sl = slice(h * HEAD, (h + 1) * HEAD)
        o, st_new = _hgrn_chunk(q_ref[:, sl], g_ref[:, sl], v_ref[:, sl], s0_ref[h].T, c)
        o_ref[:, sl] = _hgrn_out(o, gn_ref[h:h + 1, :], og_ref[:, sl])
        s_ref[h] = st_new.T


def _hgrn_sample(pack, gnorm, s0, layer, o_buf, seq, batch, heads, c):
    w = heads * HEAD
    r0 = seq // c
    blk = lambda off: pl.BlockSpec((c, w), lambda b: (r0 + b, off))
    return pl.pallas_call(
        functools.partial(_hgrn_sample_kernel, c=c, heads=heads),
        grid=(batch,),
        in_specs=[blk(0), blk(1), blk(2), blk(3),
                  _wspec((layer,), (heads, HEAD), lambda b: (0, 0)),
                  _wspec((layer,), (None, heads, HEAD, HEAD), lambda b: (b, 0, 0, 0)),
                  pl.BlockSpec(memory_space=pl.ANY)],
        out_specs=[pl.BlockSpec((c, w), lambda b: (r0 + b, 0)),
                   pl.BlockSpec((None, heads, HEAD, HEAD), lambda b: (b, 0, 0, 0))],
        out_shape=[jax.ShapeDtypeStruct(o_buf.shape, F32),
                   jax.ShapeDtypeStruct(s0.shape[1:], F32)],
        input_output_aliases={6: 0},
        compiler_params=_cparams(("parallel",)),
        name="hgrn_sample",
    )(pack, pack, pack, pack, gnorm, s0, o_buf)


def _topk_mask(score, valid, axis):
    idx = lax.broadcasted_iota(jnp.int32, score.shape, axis)
    big = jnp.int32(2 ** 30)
    sg = jnp.where(valid, score, -jnp.inf)
    sel = jnp.zeros(score.shape, F32)
    for _ in range(MOBA_TOPK):
        mx = jnp.max(sg, axis=axis, keepdims=True)
        first = jnp.min(jnp.where(sg == mx, idx, big), axis=axis, keepdims=True)
        pick = jnp.logical_and(idx == first, mx > -jnp.inf)
        sel = jnp.where(pick, 1.0, sel)
        sg = jnp.where(pick, -jnp.inf, sg)
    return sel


def _moba_prompt_kernel(it_ref, jt_ref, last_ref, q_ref, k_ref, vt_ref, o_ref,
                        qs_ref, m_ref, l_ref, acc_ref, sel_ref, km_ref, *, heads, nbp):
    s = pl.program_id(0)
    i = it_ref[s]
    j = jt_ref[s]
    blk = MOBA_BLOCK
    scale = HEAD ** -0.5

    @pl.when(s == 0)
    def _():
        km_ref[...] = jnp.zeros_like(km_ref)

    k = k_ref[...]
    kb = k.astype(BF16)
    hs = [slice(h * HEAD, (h + 1) * HEAD) for h in range(heads)]

    @pl.when(j == i)
    def _():
        km_ref[pl.ds(i, 1), :] = jnp.mean(k, axis=0, keepdims=True)
        qt = q_ref[...].T
        qs_ref[...] = (qt * (scale * LOG2E)).astype(BF16)
        key = lax.broadcasted_iota(jnp.int32, (blk, blk), 0)
        qry = lax.broadcasted_iota(jnp.int32, (blk, blk), 1)
        causal = jnp.where(key <= qry, 0.0, NEG)
        bidx = lax.broadcasted_iota(jnp.int32, (nbp, blk), 0)
        for h, sl in enumerate(hs):
            gate = _dot3(km_ref[:, sl], qt[sl, :])
            sel_ref[h] = _topk_mask(gate, bidx < i, 0)
            st = _dot(kb[:, sl], qs_ref[sl, :]) + causal
            m = jnp.max(st, axis=0, keepdims=True)
            p = jnp.exp2(st - m)
            m_ref[h:h + 1, :] = m
            l_ref[h:h + 1, :] = jnp.sum(p, axis=0, keepdims=True)
            acc_ref[sl, :] = _dot(vt_ref[sl, :], p.astype(BF16))

    @pl.when(j != i)
    def _():
        m_all = m_ref[...]
        l_all = l_ref[...]
        sel_rows = sel_ref[:, pl.ds(j, 1), :]
        sts = [_dot(kb[:, sl], qs_ref[sl, :]) for sl in hs]
        m_rows, l_rows, ps, alphas = [], [], [], []
        for h in range(heads):
            st = sts[h] + (sel_rows[h] - 1.0) * (-NEG)
            m_old = m_all[h:h + 1, :]
            m_new = jnp.maximum(m_old, jnp.max(st, axis=0, keepdims=True))
            alpha = jnp.exp2(m_old - m_new)
            p = jnp.exp2(st - m_new)
            m_rows.append(m_new)
            l_rows.append(alpha * l_all[h:h + 1, :] + jnp.sum(p, axis=0, keepdims=True))
            ps.append(p.astype(BF16))
            alphas.append(alpha)
        for h, sl in enumerate(hs):
            acc_ref[sl, :] = alphas[h] * acc_ref[sl, :] + _dot(vt_ref[sl, :], ps[h])
        m_ref[...] = jnp.concatenate(m_rows, axis=0)
        l_ref[...] = jnp.concatenate(l_rows, axis=0)

    @pl.when(last_ref[s] == 1)
    def _():
        for h in range(heads):
            sl = slice(h * HEAD, (h + 1) * HEAD)
            o_ref[:, sl] = (acc_ref[sl, :] / l_ref[h:h + 1, :]).T


def _moba_prompt(qk, vt, seq, heads):
    m = qk.shape[0]
    w = heads * HEAD
    blk = MOBA_BLOCK
    nb = seq // blk
    nbp = -(-nb // 8) * 8
    it, jt, last = [], [], []
    for i in range(nb):
        for j in [i] + list(range(i)):
            it.append(i)
            jt.append(j)
            last.append(0)
        last[-1] = 1
    tabs = [jnp.asarray(np.asarray(t, np.int32)) for t in (it, jt, last)]
    grid_spec = pltpu.PrefetchScalarGridSpec(
        num_scalar_prefetch=3,
        grid=(len(it),),
        in_specs=[pl.BlockSpec((blk, w), lambda s, it, jt, la: (it[s], 0)),
                  pl.BlockSpec((blk, w), lambda s, it, jt, la: (jt[s], 1)),
                  pl.BlockSpec((w, blk), lambda s, it, jt, la: (0, jt[s]))],
        out_specs=pl.BlockSpec((blk, w), lambda s, it, jt, la: (it[s], 0)),
        scratch_shapes=[pltpu.VMEM((w, blk), BF16),
                        pltpu.VMEM((heads, blk), F32),
                        pltpu.VMEM((heads, blk), F32),
                        pltpu.VMEM((w, blk), F32),
                        pltpu.VMEM((heads, nbp, blk), F32),
                        pltpu.VMEM((nbp, w), F32)],
    )
    return pl.pallas_call(
        functools.partial(_moba_prompt_kernel, heads=heads, nbp=nbp),
        grid_spec=grid_spec,
        out_shape=jax.ShapeDtypeStruct((m, w), F32),
        compiler_params=_cparams(("arbitrary",)),
        name="moba_prompt",
    )(*tabs, qk, qk, vt)


def _moba_sample_kernel(pt_ref, *refs, heads, t_new, pps, nb, page):
    del pt_ref
    k_refs = refs[:pps]
    v_refs = refs[pps:2 * pps]
    q_ref, kn_ref, vn_ref, obuf_ref, o_ref, m_ref, l_ref, ob_ref, km_ref = refs[2 * pps:]
    del obuf_ref
    step = pl.program_id(1)
    ppb = MOBA_BLOCK // page
    bps = pps // ppb
    rows = heads * t_new
    cols = ppb * page * heads

    q = q_ref[...]
    q_all = jnp.concatenate([q[:, h * HEAD:(h + 1) * HEAD] for h in range(heads)], axis=0)
    qs = (q_all * (HEAD ** -0.5 * LOG2E)).astype(BF16)

    @pl.when(step == 0)
    def _():
        km_ref[...] = jnp.zeros_like(km_ref)

    rh = lax.broadcasted_iota(jnp.int32, (rows, cols), 0) // t_new
    ch = jnp.bitwise_and(lax.broadcasted_iota(jnp.int32, (rows, cols), 1), heads - 1)
    head_bias = jnp.where(rh == ch, 0.0, NEG)

    blocks = [range(bi * ppb, (bi + 1) * ppb) for bi in range(bps)]
    scs = []
    for pages in blocks:
        k2 = jnp.concatenate([k_refs[p][...].reshape(page * heads, HEAD) for p in pages], axis=0)
        scs.append(_dot_nt(qs, k2.astype(BF16)))
    ms, ls, ps = [], [], []
    for sc in scs:
        sc = sc + head_bias
        m = sc.max(axis=-1, keepdims=True)
        p_exp = jnp.exp2(sc - m)
        ms.append(m)
        ls.append(p_exp.sum(axis=-1, keepdims=True))
        ps.append(p_exp.astype(BF16))
    for bi, pages in enumerate(blocks):
        jb = step * bps + bi
        v2 = jnp.concatenate([v_refs[p][...].reshape(page * heads, HEAD) for p in pages], axis=0)
        ob_ref[jb] = _dot(ps[bi], v2.astype(BF16))
        m_ref[jb] = jnp.broadcast_to(ms[bi], (rows, LANES))
        l_ref[jb] = jnp.broadcast_to(ls[bi], (rows, LANES))
        ksum = k_refs[pages[0]][...].sum(axis=0)
        for p in pages[1:]:
            ksum = ksum + k_refs[p][...].sum(axis=0)
        km_ref[jb] = ksum * (1.0 / MOBA_BLOCK)

    @pl.when(step == pl.num_programs(1) - 1)
    def _():
        tq = lax.broadcasted_iota(jnp.int32, (t_new, LANES), 0)
        kk = lax.broadcasted_iota(jnp.int32, (t_new, LANES), 1)
        causal = jnp.where(kk <= tq, 0.0, NEG)
        zpad = jnp.zeros((LANES - t_new, HEAD), F32)
        m_o, l_o, o_o = [], [], []
        for h in range(heads):
            sl = slice(h * HEAD, (h + 1) * HEAD)
            k_own = jnp.concatenate([kn_ref[:, sl], zpad], axis=0).astype(BF16)
            v_own = jnp.concatenate([vn_ref[:, sl], zpad], axis=0).astype(BF16)
            s_o = _dot_nt(qs[h * t_new:(h + 1) * t_new, :], k_own) + causal
            mh = s_o.max(axis=-1, keepdims=True)
            ph = jnp.exp2(s_o - mh)
            m_o.append(mh)
            l_o.append(ph.sum(axis=-1, keepdims=True))
            o_o.append(_dot(ph.astype(BF16), v_own))
        m_o, l_o, o_o = (jnp.concatenate(x, axis=0) for x in (m_o, l_o, o_o))
        gate = _dot3_nt(q_all, km_ref[...].reshape(nb * heads, HEAD))
        gr = lax.broadcasted_iota(jnp.int32, (rows, nb * heads), 0) // t_new
        gc = jnp.bitwise_and(lax.broadcasted_iota(jnp.int32, (rows, nb * heads), 1), heads - 1)
        sel = _topk_mask(gate, gr == gc, 1)
        picked = [sel[:, jb * heads:(jb + 1) * heads].max(axis=-1, keepdims=True) > 0.5 for jb in range(nb)]
        m_all = m_o
        for jb in range(nb):
            m_all = jnp.maximum(m_all, jnp.where(picked[jb], m_ref[jb][:, 0:1], NEG))
        w_o = jnp.exp2(m_o - m_all)
        l_tot = w_o * l_o
        o_tot = w_o * o_o
        for jb in range(nb):
            wj = jnp.where(picked[jb], jnp.exp2(jnp.where(picked[jb], m_ref[jb][:, 0:1], NEG) - m_all), 0.0)
            l_tot = l_tot + wj * l_ref[jb][:, 0:1]
            o_tot = o_tot + wj * ob_ref[jb]
        res = o_tot / l_tot
        for h in range(heads):
            o_ref[:, h * HEAD:(h + 1) * HEAD] = res[h * t_new:(h + 1) * t_new, :]


def _moba_sample(cache_k, cache_v, layer, page_table, qk, v, o_buf, seq, t_new, pps):
    depth, n_pool, page, heads, _ = cache_k.shape
    batch, n_pages = page_table.shape
    assert heads & (heads - 1) == 0
    w = heads * HEAD
    nb = n_pages * page // MOBA_BLOCK
    n_steps = n_pages // pps
    r0 = seq // t_new
    rows = heads * t_new

    def page_spec(p):
        return pl.BlockSpec((None, None, page, heads, HEAD),
                            lambda b, t, pt: (layer, pt[b * n_pages + t * pps + p], 0, 0, 0))

    grid_spec = pltpu.PrefetchScalarGridSpec(
        num_scalar_prefetch=1,
        grid=(batch, n_steps),
        in_specs=([page_spec(p) for p in range(pps)] + [page_spec(p) for p in range(pps)] +
                  [pl.BlockSpec((t_new, w), lambda b, t, pt: (r0 + b, 0)),
                   pl.BlockSpec((t_new, w), lambda b, t, pt: (r0 + b, 1)),
                   pl.BlockSpec((t_new, w), lambda b, t, pt: (r0 + b, 0)),
                   pl.BlockSpec(memory_space=pl.ANY)]),
        out_specs=pl.BlockSpec((t_new, w), lambda b, t, pt: (r0 + b, 0)),
        scratch_shapes=[pltpu.VMEM((nb, rows, LANES), F32),
                        pltpu.VMEM((nb, rows, LANES), F32),
                        pltpu.VMEM((nb, rows, HEAD), F32),
                        pltpu.VMEM((nb, heads, HEAD), F32)],
    )
    n_in = 1 + 2 * pps + 4
    return pl.pallas_call(
        functools.partial(_moba_sample_kernel, heads=heads, t_new=t_new, pps=pps, nb=nb, page=page),
        grid_spec=grid_spec,
        out_shape=jax.ShapeDtypeStruct(o_buf.shape, F32),
        input_output_aliases={n_in - 1: 0},
        compiler_params=_cparams(("parallel", "arbitrary")),
        name="moba_sample",
    )(page_table.reshape(-1), *([cache_k] * pps), *([cache_v] * pps), qk, qk, v, o_buf)


def _merge_kernel(or_ref, oa_ref, gr_ref, ga_ref, wr_ref, wa_ref, o_ref):
    yr = _dot(or_ref[...].astype(BF16), wr_ref[...])
    ya = _dot(oa_ref[...].astype(BF16), wa_ref[...])
    o_ref[...] = (gr_ref[...].astype(F32) * yr + ga_ref[...].astype(F32) * ya).astype(BF16)


def _merge(o_r, o_a, gates, w_r, w_a, layer, bm):
    m, k = o_r.shape
    d = w_r.shape[-1]
    return pl.pallas_call(
        _merge_kernel,
        grid=(m // bm,),
        in_specs=[pl.BlockSpec((bm, k), lambda i: (i, 0)),
                  pl.BlockSpec((bm, k), lambda i: (i, 0)),
                  pl.BlockSpec((bm, d), lambda i: (i, 0)),
                  pl.BlockSpec((bm, d), lambda i: (i, 1)),
                  _wspec((layer,), (k, d), lambda i: (0, 0), resident=True),
                  _wspec((layer,), (k, d), lambda i: (0, 0), resident=True)],
        out_specs=pl.BlockSpec((bm, d), lambda i: (i, 0)),
        out_shape=jax.ShapeDtypeStruct((m, d), BF16),
        compiler_params=_cparams(("parallel",)),
        name="merge",
    )(o_r, o_a, gates, gates, w_r, w_a)


def _cross_head(q, k, v):
    s = _dot_nt(q.astype(BF16), k.astype(BF16))
    p = jnp.exp(s - jnp.max(s, axis=-1, keepdims=True))
    l = jnp.sum(p, axis=-1, keepdims=True)
    return _dot(p.astype(BF16), v.astype(BF16)) / l


def _cross_prompt_kernel(q_ref, k_ref, v_ref, o_ref, *, heads):
    for h in range(heads):
        sl = slice(h * HEAD, (h + 1) * HEAD)
        o_ref[:, sl] = _cross_head(q_ref[:, sl], k_ref[:, sl], v_ref[:, sl])


def _cross_sample_kernel(q_ref, k_ref, v_ref, obuf_ref, o_ref, *, heads):
    del obuf_ref
    for h in range(heads):
        sl = slice(h * HEAD, (h + 1) * HEAD)
        o_ref[:, sl] = _cross_head(q_ref[:, sl], k_ref[:, h, :], v_ref[:, h, :])


def _cross_prompt(qc, mk, mv, seq, heads, bq):
    m, w = qc.shape
    n = mk.shape[0]
    return pl.pallas_call(
        functools.partial(_cross_prompt_kernel, heads=heads),
        grid=(seq // bq,),
        in_specs=[pl.BlockSpec((bq, w), lambda i: (i, 0)),
                  pl.BlockSpec((n, w), lambda i: (0, 0)),
                  pl.BlockSpec((n, w), lambda i: (0, 0))],
        out_specs=pl.BlockSpec((bq, w), lambda i: (i, 0)),
        out_shape=jax.ShapeDtypeStruct((m, w), F32),
        compiler_params=_cparams(("parallel",)),
        name="cross_prompt",
    )(qc, mk, mv)


def _cross_sample(qc, mem_k, mem_v, layer, o_buf, seq, t_new):
    depth, batch, n, heads, _ = mem_k.shape
    w = heads * HEAD
    r0 = seq // t_new
    mem_spec = pl.BlockSpec((None, None, n, heads, HEAD), lambda b: (layer, b, 0, 0, 0))
    return pl.pallas_call(
        functools.partial(_cross_sample_kernel, heads=heads),
        grid=(batch,),
        in_specs=[pl.BlockSpec((t_new, w), lambda b: (r0 + b, 0)), mem_spec, mem_spec,
                  pl.BlockSpec(memory_space=pl.ANY)],
        out_specs=pl.BlockSpec((t_new, w), lambda b: (r0 + b, 0)),
        out_shape=jax.ShapeDtypeStruct(o_buf.shape, F32),
        input_output_aliases={3: 0},
        compiler_params=_cparams(("parallel",)),
        name="cross_sample",
    )(qc, mem_k, mem_v, o_buf)


def _rope_tables(seq, past_len, batch, t_new):
    half = HEAD // 2
    inv_freq = ROPE_THETA ** (-jnp.arange(half, dtype=F32) / half)
    pos = jnp.concatenate([jnp.arange(seq, dtype=jnp.int32),
                           jnp.tile(past_len + jnp.arange(t_new, dtype=jnp.int32), batch)])
    ang = pos.astype(F32)[:, None] * inv_freq[None, :]
    cos = jnp.cos(ang)
    sin = jnp.sin(ang)
    return jnp.concatenate([cos, cos], axis=-1), jnp.concatenate([-sin, sin], axis=-1)


def kernel(x_prompt, x_sample, cache_k, cache_v, state_hgrn, cache_mem_k, cache_mem_v, page_table, mem_prompt,
           norm_pre, norm_post, ffn_gate, ffn_up, ffn_down, w_in, lower_bounds, hgrn_gnorm,
           w_branch_r, w_branch_a, w_out, mem_norm, w_cq, w_ck, w_cv, w_co):
    b_p, seq, d = x_prompt.shape
    batch, t_new, _ = x_sample.shape
    depth, n_pool, page, h_a, _ = cache_k.shape
    h_r = state_hgrn.shape[2]
    h_m = cache_mem_k.shape[3]
    n_mem = cache_mem_k.shape[2]
    n_pages = page_table.shape[1]
    past_len = n_pages * page
    assert b_p == 1 and h_r == 8 and h_a == 8
    assert seq % MOBA_BLOCK == 0 and past_len % MOBA_BLOCK == 0 and t_new <= MOBA_BLOCK
    assert t_new % 8 == 0 and seq % t_new == 0 and MOBA_BLOCK % page == 0
    m = seq + batch * t_new
    w8 = 8 * HEAD

    bm = _pick(m, (768, 512, 384, 256, 128))
    bm_res = _pick(m, (256, 128))
    bn_ff = _pick(ffn_gate.shape[-1], (512, 256))
    hg_c = _pick(seq, (256,))
    pps = _pick(n_pages, (8, 4, 2))

    cos_t, sin_t = _rope_tables(seq, past_len, batch, t_new)
    lower_bounds = lower_bounds.astype(F32)
    bf = lambda a: a.astype(BF16)
    ffn_gate, ffn_up, ffn_down, w_in = bf(ffn_gate), bf(ffn_up), bf(ffn_down), bf(w_in)
    w_branch_r, w_branch_a, w_out = bf(w_branch_r), bf(w_branch_a), bf(w_out)
    w_cq, w_ck, w_cv, w_co = bf(w_cq), bf(w_ck), bf(w_cv), bf(w_co)

    x = jnp.concatenate([x_prompt.reshape(seq, d), x_sample.reshape(batch * t_new, d)], axis=0)
    u = _norm_cast(x, norm_pre[0, 0][None], bm)

    outs = {k: [] for k in ("kp", "vp", "sp", "mkp", "mvp", "ks", "vs", "ss")}
    for l in range(depth):
        g_pre = lambda i: norm_pre[l, i][None]
        g_post = lambda i: norm_post[l, i][None]
        g_next_layer = norm_pre[l + 1, 0][None] if l + 1 < depth else jnp.ones((1, d), F32)

        mk_p, mv_p = _memkv(mem_prompt.reshape(n_mem, d), mem_norm[l][None], w_ck, w_cv, l)

        h = _ffn_up(u, ffn_gate, ffn_up, (l, 0), bm, bn_ff)
        x, u = _proj_res(h, ffn_down, (l, 0), x, g_post(0), g_pre(1), 0.5, bm_res)

        pack = _win_hgrn(u, w_in, lower_bounds, l, bm)
        qk = _win_qk(u, w_in, l, cos_t, sin_t, 4, h_a, bm)
        v, vt = _win_v(u, w_in, l, 6, h_a, bm)
        gates = _win_gates(u, w_in, l, 7, 2 * d // w8, bm)

        o_r, s_p = _hgrn_prompt(pack, hgrn_gnorm, l, seq, h_r, hg_c)
        o_r, s_s = _hgrn_sample(pack, hgrn_gnorm, state_hgrn, l, o_r, seq, batch, h_r, t_new)

        o_a = _moba_prompt(qk, vt, seq, h_a)
        o_a = _moba_sample(cache_k, cache_v, l, page_table, qk, v, o_a, seq, t_new, pps)

        merged = _merge(o_r, o_a, gates, w_branch_r, w_branch_a, l, bm_res)
        x, u = _proj_res(merged, w_out, (l,), x, g_post(1), g_pre(2), 1.0, bm_res)

        qc = _scaled_proj(u, w_cq, (l,), HEAD ** -0.5, bm)
        oc = _cross_prompt(qc, mk_p, mv_p, seq, h_m, _pick(seq, (512, 256)))
        oc = _cross_sample(qc, cache_mem_k, cache_mem_v, l, oc, seq, t_new)
        x, u = _proj_res(oc, w_co, (l,), x, g_post(2), g_pre(3), 1.0, bm_res)

        h = _ffn_up(u, ffn_gate, ffn_up, (l, 1), bm, bn_ff)
        x, u = _proj_res(h, ffn_down, (l, 1), x, g_post(3), g_next_layer, 0.5, bm_res)

        k_all = qk[:, w8:]
        outs["kp"].append(k_all[:seq].reshape(1, seq, h_a, HEAD))
        outs["vp"].append(v[:seq].reshape(1, seq, h_a, HEAD))
        outs["sp"].append(s_p.reshape(1, h_r, HEAD, HEAD))
        outs["mkp"].append(mk_p.reshape(1, n_mem, h_m, HEAD))
        outs["mvp"].append(mv_p.reshape(1, n_mem, h_m, HEAD))
        outs["ks"].append(k_all[seq:].reshape(batch, t_new, h_a, HEAD))
        outs["vs"].append(v[seq:].reshape(batch, t_new, h_a, HEAD))
        outs["ss"].append(s_s)

    st = lambda k: jnp.stack(outs[k])
    return (x[:seq].reshape(1, seq, d), x[seq:].reshape(batch, t_new, d),
            st("kp"), st("vp"), st("sp"), st("mkp"), st("mvp"), st("ks"), st("vs"), st("ss"))
```

```python
import functools

import numpy as np
import jax
import jax.numpy as jnp
from jax import lax
from jax.experimental import pallas as pl
from jax.experimental.pallas import tpu as pltpu

F32 = jnp.float32
BF16 = jnp.bfloat16

MOBA_BLOCK = 256
MOBA_TOPK = 3
ROPE_THETA = 10000.0
EPS = 1e-6
HEAD = 128
LANES = 128
NEG = -1e30
LOG2E = 1.4426950408889634
VMEM_LIMIT = 56 * 1024 * 1024


def _cparams(sem, vmem=VMEM_LIMIT):
    return pltpu.CompilerParams(dimension_semantics=sem, vmem_limit_bytes=vmem)


def _pick(n, cands):
    for c in cands:
        if n % c == 0:
            return c
    raise ValueError(f"no tile in {cands} divides {n}")


def _wspec(lead, shape, index_map, resident=False):
    full_shape = (None,) * len(lead) + tuple(shape)
    full_map = lambda *a: tuple(lead) + tuple(index_map(*a))
    if resident:
        return pl.BlockSpec(full_shape, full_map, pipeline_mode=pl.Buffered(1))
    return pl.BlockSpec(full_shape, full_map)


def _two_src_specs(bm, k, n_p):
    return [pl.BlockSpec((bm, k), lambda i: (jnp.minimum(i, n_p - 1), 0)),
            pl.BlockSpec((bm, k), lambda i: (jnp.maximum(i - n_p, 0), 0))]


def _two_src_load(refs, n_p):
    return jnp.where(pl.program_id(0) < n_p, refs[0][...], refs[1][...])


def _dot(a, b):
    return jnp.dot(a, b, preferred_element_type=F32)


def _dot_nt(a, b):
    return lax.dot_general(a, b, (((1,), (1,)), ((), ())), preferred_element_type=F32)


def _split2(a):
    hi = a.astype(BF16)
    lo = (a - hi.astype(F32)).astype(BF16)
    return hi, lo


def _dot3_nt(a, b):
    ah, al = _split2(a)
    bh, bl = _split2(b)
    return _dot_nt(ah, bh) + _dot_nt(ah, bl) + _dot_nt(al, bh)


def _dot3(a, b):
    ah, al = _split2(a)
    bh, bl = _split2(b)
    return _dot(ah, bh) + _dot(ah, bl) + _dot(al, bh)


def _sigmoid(x):
    return 1.0 / (1.0 + jnp.exp(-x))


def _silu(x):
    return x * _sigmoid(x)


def _rms(y, g):
    ms = jnp.mean(y * y, axis=-1, keepdims=True)
    return y * lax.rsqrt(ms + EPS) * g


def _cast_once(w_ref, wb_ref, axis):
    @pl.when(pl.program_id(axis) == 0)
    def _():
        wb_ref[...] = w_ref[...].astype(BF16)


def _norm_kernel(xp_ref, xs_ref, g_ref, u_ref, *, n_p):
    u_ref[...] = _rms(_two_src_load((xp_ref, xs_ref), n_p), g_ref[...]).astype(BF16)


def _norm_cast(x_p, x_s, g, bm):
    seq, d = x_p.shape
    m = seq + x_s.shape[0]
    n_p = seq // bm
    return pl.pallas_call(
        functools.partial(_norm_kernel, n_p=n_p),
        grid=(m // bm,),
        in_specs=_two_src_specs(bm, d, n_p) + [pl.BlockSpec((1, d), lambda i: (0, 0))],
        out_specs=pl.BlockSpec((bm, d), lambda i: (i, 0)),
        out_shape=jax.ShapeDtypeStruct((m, d), BF16),
        compiler_params=_cparams(("parallel",)),
        name="norm_cast",
    )(x_p, x_s, g)


def _ffn_up_kernel(u_ref, wg_ref, wu_ref, h_ref, wgb_ref, wub_ref):
    _cast_once(wg_ref, wgb_ref, 1)
    _cast_once(wu_ref, wub_ref, 1)
    u = u_ref[...]
    h_ref[...] = (_silu(_dot(u, wgb_ref[...])) * _dot(u, wub_ref[...])).astype(BF16)


def _ffn_up(u, wg, wu, lead, bm, bn):
    m, d = u.shape
    f = wg.shape[-1]
    return pl.pallas_call(
        _ffn_up_kernel,
        grid=(f // bn, m // bm),
        in_specs=[pl.BlockSpec((bm, d), lambda n, i: (i, 0)),
                  _wspec(lead, (d, bn), lambda n, i: (0, n)),
                  _wspec(lead, (d, bn), lambda n, i: (0, n))],
        out_specs=pl.BlockSpec((bm, bn), lambda n, i: (i, n)),
        out_shape=jax.ShapeDtypeStruct((m, f), BF16),
        scratch_shapes=[pltpu.VMEM((d, bn), BF16), pltpu.VMEM((d, bn), BF16)],
        compiler_params=_cparams(("arbitrary", "arbitrary")),
        name="ffn_up",
    )(u, wg, wu)


def _proj_res_kernel(*refs, scale, a2, x2, split, n_p):
    refs = list(refs)
    a_refs = [refs.pop(0) for _ in range(2 if a2 else 1)]
    w_ref = refs.pop(0)
    x_refs = [refs.pop(0) for _ in range(2 if x2 else 1)]
    gp_ref = refs.pop(0)
    a = _two_src_load(a_refs, n_p) if a2 else a_refs[0][...]
    x = _two_src_load(x_refs, n_p) if x2 else x_refs[0][...]
    xn = x + scale * _rms(_dot(a.astype(BF16), w_ref[...]), gp_ref[...])
    if split:
        xop_ref, xos_ref = refs
        is_p = pl.program_id(0) < n_p

        @pl.when(is_p)
        def _():
            xop_ref[...] = xn

        @pl.when(jnp.logical_not(is_p))
        def _():
            xos_ref[...] = xn
    else:
        gn_ref, xo_ref, uo_ref = refs
        xo_ref[...] = xn
        uo_ref[...] = _rms(xn, gn_ref[...]).astype(BF16)


def _proj_res(a, w, lead, x, g_post, g_next, scale, bm, seq):
    a2, x2, split = isinstance(a, tuple), isinstance(x, tuple), g_next is None
    k, d = w.shape[-2:]
    n_p = seq // bm
    m = (x[0].shape[0] + x[1].shape[0]) if x2 else x.shape[0]
    row = lambda c: pl.BlockSpec((bm, c), lambda i: (i, 0))
    vec = pl.BlockSpec((1, d), lambda i: (0, 0))
    in_specs = (_two_src_specs(bm, k, n_p) if a2 else [row(k)])
    in_specs += [_wspec(lead, (k, d), lambda i: (0, 0), resident=True)]
    in_specs += (_two_src_specs(bm, d, n_p) if x2 else [row(d)]) + [vec]
    args = (list(a) if a2 else [a]) + [w] + (list(x) if x2 else [x]) + [g_post]
    if split:
        out_specs = _two_src_specs(bm, d, n_p)
        out_shape = [jax.ShapeDtypeStruct((seq, d), F32), jax.ShapeDtypeStruct((m - seq, d), F32)]
    else:
        in_specs += [vec]
        args += [g_next]
        out_specs = [row(d), row(d)]
        out_shape = [jax.ShapeDtypeStruct((m, d), F32), jax.ShapeDtypeStruct((m, d), BF16)]
    return pl.pallas_call(
        functools.partial(_proj_res_kernel, scale=scale, a2=a2, x2=x2, split=split, n_p=n_p),
        grid=(m // bm,),
        in_specs=in_specs,
        out_specs=out_specs,
        out_shape=out_shape,
        compiler_params=_cparams(("arbitrary",)),
        name="proj_res",
    )(*args)


def _win_hgrn_kernel(u_ref, w_ref, lb_ref, o_ref, wb_ref, *, layer):
    _cast_once(w_ref, wb_ref, 1)
    n = pl.program_id(0)
    acc = _dot(u_ref[...], wb_ref[...])

    @pl.when(n == 0)
    def _():
        o_ref[...] = _silu(acc) * (HEAD ** -0.5)

    @pl.when(n == 1)
    def _():
        lbs = lb_ref[...]
        e = jnp.exp(lbs - jnp.max(lbs, axis=0, keepdims=True))
        sm = e / jnp.sum(e, axis=0, keepdims=True)
        cum = sm[0:1, :]
        for i in range(1, layer + 1):
            cum = cum + sm[i:i + 1, :]
        lb = cum - sm[0:1, :]
        ls = jnp.minimum(acc, 0.0) - jnp.log1p(jnp.exp(-jnp.abs(acc)))
        a = jnp.log(lb)
        b = jnp.log1p(-lb) + ls
        o_ref[...] = jnp.maximum(a, b) + jnp.log1p(jnp.exp(-jnp.abs(a - b)))

    @pl.when(n == 2)
    def _():
        o_ref[...] = acc

    @pl.when(n == 3)
    def _():
        o_ref[...] = _silu(acc)


def _win_hgrn(u, w_in, lower_bounds, layer, bm):
    m, d = u.shape
    w = 8 * HEAD
    depth = lower_bounds.shape[0]
    return pl.pallas_call(
        functools.partial(_win_hgrn_kernel, layer=layer),
        grid=(4, m // bm),
        in_specs=[pl.BlockSpec((bm, d), lambda n, i: (i, 0)),
                  _wspec((layer,), (d, w), lambda n, i: (0, n)),
                  pl.BlockSpec((depth, w), lambda n, i: (0, 0))],
        out_specs=pl.BlockSpec((bm, w), lambda n, i: (i, n)),
        out_shape=jax.ShapeDtypeStruct((m, 4 * w), F32),
        scratch_shapes=[pltpu.VMEM((d, w), BF16)],
        compiler_params=_cparams(("arbitrary", "arbitrary")),
        name="win_hgrn",
    )(u, w_in, lower_bounds)


def _win_qk_kernel(u_ref, w_ref, cos_ref, sin_ref, o_ref, wb_ref, *, heads):
    _cast_once(w_ref, wb_ref, 1)
    acc = _dot(u_ref[...], wb_ref[...])
    c = cos_ref[...]
    s = sin_ref[...]
    for h in range(heads):
        xh = acc[:, h * HEAD:(h + 1) * HEAD]
        o_ref[:, h * HEAD:(h + 1) * HEAD] = xh * c + pltpu.roll(xh, HEAD // 2, 1) * s


def _win_qk(u, w_in, layer, cos_t, sin_t, col_blk, heads, bm):
    m, d = u.shape
    w = heads * HEAD
    return pl.pallas_call(
        functools.partial(_win_qk_kernel, heads=heads),
        grid=(2, m // bm),
        in_specs=[pl.BlockSpec((bm, d), lambda n, i: (i, 0)),
                  _wspec((layer,), (d, w), lambda n, i: (0, col_blk + n)),
                  pl.BlockSpec((bm, HEAD), lambda n, i: (i, 0)),
                  pl.BlockSpec((bm, HEAD), lambda n, i: (i, 0))],
        out_specs=pl.BlockSpec((bm, w), lambda n, i: (i, n)),
        out_shape=jax.ShapeDtypeStruct((m, 2 * w), F32),
        scratch_shapes=[pltpu.VMEM((d, w), BF16)],
        compiler_params=_cparams(("arbitrary", "arbitrary")),
        name="win_qk",
    )(u, w_in, cos_t, sin_t)


def _win_v_kernel(u_ref, w_ref, v_ref, vt_ref, wb_ref):
    _cast_once(w_ref, wb_ref, 0)
    acc = _dot(u_ref[...], wb_ref[...])
    v_ref[...] = acc
    vt_ref[...] = acc.T.astype(BF16)


def _win_v(u, w_in, layer, col_blk, heads, bm):
    m, d = u.shape
    w = heads * HEAD
    return pl.pallas_call(
        _win_v_kernel,
        grid=(m // bm,),
        in_specs=[pl.BlockSpec((bm, d), lambda i: (i, 0)),
                  _wspec((layer,), (d, w), lambda i: (0, col_blk), resident=True)],
        out_specs=[pl.BlockSpec((bm, w), lambda i: (i, 0)), pl.BlockSpec((w, bm), lambda i: (0, i))],
        out_shape=[jax.ShapeDtypeStruct((m, w), F32), jax.ShapeDtypeStruct((w, m), BF16)],
        scratch_shapes=[pltpu.VMEM((d, w), BF16)],
        compiler_params=_cparams(("arbitrary",)),
        name="win_v",
    )(u, w_in)


def _win_gate_kernel(u_ref, w_ref, o_ref, wb_ref):
    _cast_once(w_ref, wb_ref, 1)
    o_ref[...] = _sigmoid(_dot(u_ref[...], wb_ref[...])).astype(BF16)


def _win_gates(u, w_in, layer, col_blk, n_blk, bm):
    m, d = u.shape
    w = 8 * HEAD
    return pl.pallas_call(
        _win_gate_kernel,
        grid=(n_blk, m // bm),
        in_specs=[pl.BlockSpec((bm, d), lambda n, i: (i, 0)),
                  _wspec((layer,), (d, w), lambda n, i: (0, col_blk + n))],
        out_specs=pl.BlockSpec((bm, w), lambda n, i: (i, n)),
        out_shape=jax.ShapeDtypeStruct((m, n_blk * w), BF16),
        scratch_shapes=[pltpu.VMEM((d, w), BF16)],
        compiler_params=_cparams(("arbitrary", "arbitrary")),
        name="win_gates",
    )(u, w_in)


def _scaled_proj_kernel(u_ref, w_ref, o_ref, *, scale):
    o_ref[...] = _dot(u_ref[...], w_ref[...]) * scale


def _scaled_proj(u, w, lead, scale, bm):
    m, d = u.shape
    n = w.shape[-1]
    return pl.pallas_call(
        functools.partial(_scaled_proj_kernel, scale=scale),
        grid=(m // bm,),
        in_specs=[pl.BlockSpec((bm, d), lambda i: (i, 0)),
                  _wspec(lead, (d, n), lambda i: (0, 0), resident=True)],
        out_specs=pl.BlockSpec((bm, n), lambda i: (i, 0)),
        out_shape=jax.ShapeDtypeStruct((m, n), F32),
        compiler_params=_cparams(("parallel",)),
        name="scaled_proj",
    )(u, w)


def _memkv_kernel(mem_ref, g_ref, wk_ref, wv_ref, k_ref, v_ref):
    mn = _rms(mem_ref[...], g_ref[...]).astype(BF16)
    k_ref[...] = _dot(mn, wk_ref[...])
    v_ref[...] = _dot(mn, wv_ref[...])


def _memkv(mem, g, wk, wv, layer):
    n, d = mem.shape
    w = wk.shape[-1]
    return pl.pallas_call(
        _memkv_kernel,
        grid=(1,),
        in_specs=[pl.BlockSpec((n, d), lambda i: (0, 0)),
                  pl.BlockSpec((1, d), lambda i: (0, 0)),
                  _wspec((layer,), (d, w), lambda i: (0, 0)),
                  _wspec((layer,), (d, w), lambda i: (0, 0))],
        out_specs=[pl.BlockSpec((n, w), lambda i: (0, 0)), pl.BlockSpec((n, w), lambda i: (0, 0))],
        out_shape=[jax.ShapeDtypeStruct((n, w), F32), jax.ShapeDtypeStruct((n, w), F32)],
        compiler_params=_cparams(("arbitrary",)),
        name="memkv",
    )(mem, g, wk, wv)


def _cumsum_rows(g, c):
    row = lax.broadcasted_iota(jnp.int32, g.shape, 0)
    out = g
    sh = 1
    while sh < c:
        out = out + jnp.where(row >= sh, pltpu.roll(out, sh, 0), 0.0)
        sh *= 2
    return out


def _hgrn_chunk(q, g, v, st, c):
    k = 1.0 - jnp.exp(g)
    gc = _cumsum_rows(g, c)
    row = lax.broadcasted_iota(jnp.int32, (c, HEAD), 0)
    ra = lax.broadcasted_iota(jnp.int32, (c, c), 0)
    ca = lax.broadcasted_iota(jnp.int32, (c, c), 1)
    x = jnp.bitwise_xor(ra, ca)

    a = jnp.zeros((c, c), F32)
    b = c // 2
    while b >= 1:
        upper = jnp.bitwise_and(row, b) != 0
        if b >= 4:
            nb = c // (2 * b)
            gcb = gc.reshape(nb, 2 * b, HEAD)
            r = gcb[:, b - 1:b, :]
            up3 = upper.reshape(nb, 2 * b, HEAD)
            dlt = jnp.where(up3, gcb - r, r - gcb).reshape(c, HEAD)
        elif b == 2:
            t4 = jnp.bitwise_and(row, 3)
            g_next = pltpu.roll(g, c - 1, 0)
            g_prev = pltpu.roll(g, 1, 0)
            dlt = jnp.where(t4 == 0, g_next, jnp.where(t4 == 1, 0.0, jnp.where(t4 == 2, g, g + g_prev)))
        else:
            dlt = jnp.where(upper, g, 0.0)
        z = (jnp.where(upper, q, k) * jnp.exp(dlt)).astype(BF16)
        p = _dot_nt(z, z)
        a = p if b == c // 2 else jnp.where(x < 2 * b, p, a)
        b //= 2
    a = jnp.where(ra > ca, a, 0.0)

    vb = v.astype(BF16)
    o = _dot(a.astype(BF16), vb)
    o = o + jnp.sum(q * k, axis=-1, keepdims=True) * v
    o = o + _dot_nt((q * jnp.exp(gc)).astype(BF16), st.astype(BF16))
    g_end = gc[c - 1:c, :]
    khat = (k * jnp.exp(g_end - gc)).astype(BF16)
    st_new = st * jnp.exp(g_end) + lax.dot_general(vb, khat, (((0,), (0,)), ((), ())),
                                                   preferred_element_type=F32)
    return o, st_new


def _hgrn_out(o, gn, og):
    return _rms(o, gn) * og


def _hgrn_prompt_kernel(q_ref, g_ref, v_ref, og_ref, gn_ref, o_ref, s_ref, st_ref, *, c):
    h = pl.program_id(0)
    ci = pl.program_id(1)

    @pl.when(ci == 0)
    def _():
        st_ref[...] = jnp.zeros_like(st_ref)

    o, st_new = _hgrn_chunk(q_ref[...], g_ref[...], v_ref[...], st_ref[...], c)
    st_ref[...] = st_new
    o_ref[...] = _hgrn_out(o, gn_ref[pl.ds(h, 1), :], og_ref[...])

    @pl.when(ci == pl.num_programs(1) - 1)
    def _():
        s_ref[...] = st_new.T


def _hgrn_prompt(pack, gnorm, layer, seq, heads, c):
    blk = lambda off: pl.BlockSpec((c, HEAD), lambda h, i: (i, off + h))
    return pl.pallas_call(
        functools.partial(_hgrn_prompt_kernel, c=c),
        grid=(heads, seq // c),
        in_specs=[blk(0), blk(heads), blk(2 * heads), blk(3 * heads),
                  _wspec((layer,), (heads, HEAD), lambda h, i: (0, 0))],
        out_specs=[pl.BlockSpec((c, HEAD), lambda h, i: (i, h)),
                   pl.BlockSpec((None, HEAD, HEAD), lambda h, i: (h, 0, 0))],
        out_shape=[jax.ShapeDtypeStruct((seq, heads * HEAD), F32),
                   jax.ShapeDtypeStruct((heads, HEAD, HEAD), F32)],
        scratch_shapes=[pltpu.VMEM((HEAD, HEAD), F32)],
        compiler_params=_cparams(("parallel", "arbitrary")),
        name="hgrn_prompt",
    )(pack, pack, pack, pack, gnorm)


def _hgrn_sample_kernel(q_ref, g_ref, v_ref, og_ref, gn_ref, s0_ref, o_ref, s_ref, *, c, heads):
    for h in range(heads):
        sl = slice(h * HEAD, (h + 1) * HEAD)
        o, st_new = _hgrn_chunk(q_ref[:, sl], g_ref[:, sl], v_ref[:, sl], s0_ref[h].T, c)
        o_ref[:, sl] = _hgrn_out(o, gn_ref[h:h + 1, :], og_ref[:, sl])
        s_ref[h] = st_new.T


def _hgrn_sample(pack, gnorm, s0, layer, seq, batch, heads, c):
    w = heads * HEAD
    r0 = seq // c
    blk = lambda off: pl.BlockSpec((c, w), lambda b: (r0 + b, off))
    return pl.pallas_call(
        functools.partial(_hgrn_sample_kernel, c=c, heads=heads),
        grid=(batch,),
        in_specs=[blk(0), blk(1), blk(2), blk(3),
                  _wspec((layer,), (heads, HEAD), lambda b: (0, 0)),
                  _wspec((layer,), (None, heads, HEAD, HEAD), lambda b: (b, 0, 0, 0))],
        out_specs=[pl.BlockSpec((c, w), lambda b: (b, 0)),
                   pl.BlockSpec((None, heads, HEAD, HEAD), lambda b: (b, 0, 0, 0))],
        out_shape=[jax.ShapeDtypeStruct((batch * c, w), F32),
                   jax.ShapeDtypeStruct(s0.shape[1:], F32)],
        compiler_params=_cparams(("parallel",)),
        name="hgrn_sample",
    )(pack, pack, pack, pack, gnorm, s0)


def _topk_mask(score, valid, axis):
    idx = lax.broadcasted_iota(jnp.int32, score.shape, axis)
    big = jnp.int32(2 ** 30)
    sg = jnp.where(valid, score, -jnp.inf)
    sel = jnp.zeros(score.shape, F32)
    for _ in range(MOBA_TOPK):
        mx = jnp.max(sg, axis=axis, keepdims=True)
        first = jnp.min(jnp.where(sg == mx, idx, big), axis=axis, keepdims=True)
        pick = jnp.logical_and(idx == first, mx > -jnp.inf)
        sel = jnp.where(pick, 1.0, sel)
        sg = jnp.where(pick, -jnp.inf, sg)
    return sel


def _moba_prompt_kernel(it_ref, jt_ref, fl_ref, q_ref, ko_ref, vto_ref, kp_ref, vtp_ref, o_ref,
                        qs_ref, m_ref, l_ref, acc_ref, sel_ref, km_ref, *, heads, nbp):
    s = pl.program_id(0)
    i = it_ref[s]
    jp = jt_ref[s]
    own = jnp.bitwise_and(fl_ref[s], 1) == 1
    last = jnp.bitwise_and(fl_ref[s], 2) == 2
    blk = MOBA_BLOCK
    hs = [slice(h * HEAD, (h + 1) * HEAD) for h in range(heads)]

    @pl.when(s == 0)
    def _():
        km_ref[...] = jnp.zeros_like(km_ref)

    @pl.when(own)
    def _():
        k = ko_ref[...]
        kb = k.astype(BF16)
        km_ref[pl.ds(i, 1), :] = jnp.mean(k, axis=0, keepdims=True)
        qt = q_ref[...].T
        qs_ref[...] = (qt * (HEAD ** -0.5 * LOG2E)).astype(BF16)
        key = lax.broadcasted_iota(jnp.int32, (blk, blk), 0)
        qry = lax.broadcasted_iota(jnp.int32, (blk, blk), 1)
        causal = jnp.where(key <= qry, 0.0, NEG)
        bidx = lax.broadcasted_iota(jnp.int32, (nbp, blk), 0)
        for h, sl in enumerate(hs):
            gate = _dot3(km_ref[:, sl], qt[sl, :])
            sel_ref[h] = _topk_mask(gate, bidx < i, 0)
            st = _dot(kb[:, sl], qs_ref[sl, :]) + causal
            m = jnp.max(st, axis=0, keepdims=True)
            p = jnp.exp2(st - m)
            m_ref[h:h + 1, :] = m
            l_ref[h:h + 1, :] = jnp.sum(p, axis=0, keepdims=True)
            acc_ref[sl, :] = _dot(vto_ref[sl, :], p.astype(BF16))

    @pl.when(jnp.logical_not(own))
    def _():
        kb = kp_ref[...].astype(BF16)
        m_all = m_ref[...]
        l_all = l_ref[...]
        sel_a = sel_ref[:, pl.ds(2 * jp, 1), :]
        sel_b = sel_ref[:, pl.ds(2 * jp + 1, 1), :]
        sts = [_dot(kb[:, sl], qs_ref[sl, :]) for sl in hs]
        m_rows, l_rows, ps, alphas = [], [], [], []
        for h in range(heads):
            st_a = sts[h][:blk] + (sel_a[h] - 1.0) * (-NEG)
            st_b = sts[h][blk:] + (sel_b[h] - 1.0) * (-NEG)
            m_old = m_all[h:h + 1, :]
            m_new = jnp.maximum(m_old, jnp.maximum(jnp.max(st_a, axis=0, keepdims=True),
                                                   jnp.max(st_b, axis=0, keepdims=True)))
            alpha = jnp.exp2(m_old - m_new)
            p_a = jnp.exp2(st_a - m_new)
            p_b = jnp.exp2(st_b - m_new)
            m_rows.append(m_new)
            l_rows.append(alpha * l_all[h:h + 1, :] + jnp.sum(p_a, axis=0, keepdims=True)
                          + jnp.sum(p_b, axis=0, keepdims=True))
            ps.append(jnp.concatenate([p_a.astype(BF16), p_b.astype(BF16)], axis=0))
            alphas.append(alpha)
        for h, sl in enumerate(hs):
            acc_ref[sl, :] = alphas[h] * acc_ref[sl, :] + _dot(vtp_ref[sl, :], ps[h])
        m_ref[...] = jnp.concatenate(m_rows, axis=0)
        l_ref[...] = jnp.concatenate(l_rows, axis=0)

    @pl.when(last)
    def _():
        for h, sl in enumerate(hs):
            o_ref[:, sl] = (acc_ref[sl, :] / l_ref[h:h + 1, :]).T


def _moba_prompt(qk, vt, seq, heads):
    w = heads * HEAD
    blk = MOBA_BLOCK
    nb = seq // blk
    assert nb % 2 == 0
    nbp = -(-nb // 8) * 8
    it, jt, fl = [], [], []
    for i in range(nb):
        it.append(i)
        jt.append(0)
        fl.append(1)
        for jp in range((i + 1) // 2):
            it.append(i)
            jt.append(jp)
            fl.append(0)
        fl[-1] += 2
    tabs = [jnp.asarray(np.asarray(t, np.int32)) for t in (it, jt, fl)]
    grid_spec = pltpu.PrefetchScalarGridSpec(
        num_scalar_prefetch=3,
        grid=(len(it),),
        in_specs=[pl.BlockSpec((blk, w), lambda s, it, jt, fl: (it[s], 0)),
                  pl.BlockSpec((blk, w), lambda s, it, jt, fl: (it[s], 1)),
                  pl.BlockSpec((w, blk), lambda s, it, jt, fl: (0, it[s])),
                  pl.BlockSpec((2 * blk, w), lambda s, it, jt, fl: (jt[s], 1)),
                  pl.BlockSpec((w, 2 * blk), lambda s, it, jt, fl: (0, jt[s]))],
        out_specs=pl.BlockSpec((blk, w), lambda s, it, jt, fl: (it[s], 0)),
        scratch_shapes=[pltpu.VMEM((w, blk), BF16),
                        pltpu.VMEM((heads, blk), F32),
                        pltpu.VMEM((heads, blk), F32),
                        pltpu.VMEM((w, blk), F32),
                        pltpu.VMEM((heads, nbp, blk), F32),
                        pltpu.VMEM((nbp, w), F32)],
    )
    return pl.pallas_call(
        functools.partial(_moba_prompt_kernel, heads=heads, nbp=nbp),
        grid_spec=grid_spec,
        out_shape=jax.ShapeDtypeStruct((seq, w), F32),
        compiler_params=_cparams(("arbitrary",)),
        name="moba_prompt",
    )(*tabs, qk, qk, vt, qk, vt)


def _moba_sample_kernel(pt_ref, *refs, heads, t_new, pps, nb, page):
    del pt_ref
    k_refs = refs[:pps]
    v_refs = refs[pps:2 * pps]
    q_ref, kn_ref, vn_ref, o_ref, m_ref, l_ref, ob_ref, km_ref = refs[2 * pps:]
    step = pl.program_id(1)
    ppb = MOBA_BLOCK // page
    bps = pps // ppb
    rows = heads * t_new
    cols = ppb * page * heads

    q = q_ref[...]
    q_all = jnp.concatenate([q[:, h * HEAD:(h + 1) * HEAD] for h in range(heads)], axis=0)
    qs = (q_all * (HEAD ** -0.5 * LOG2E)).astype(BF16)

    rh = lax.broadcasted_iota(jnp.int32, (rows, cols), 0) // t_new
    ch = jnp.bitwise_and(lax.broadcasted_iota(jnp.int32, (rows, cols), 1), heads - 1)
    head_bias = jnp.where(rh == ch, 0.0, NEG)

    blocks = [range(bi * ppb, (bi + 1) * ppb) for bi in range(bps)]
    scs = []
    for pages in blocks:
        k2 = jnp.concatenate([k_refs[p][...].reshape(page * heads, HEAD) for p in pages], axis=0)
        scs.append(_dot_nt(qs, k2.astype(BF16)))
    ms, ls, ps = [], [], []
    for sc in scs:
        sc = sc + head_bias
        m = sc.max(axis=-1, keepdims=True)
        p_exp = jnp.exp2(sc - m)
        ms.append(m)
        ls.append(p_exp.sum(axis=-1, keepdims=True))
        ps.append(p_exp.astype(BF16))
    for bi, pages in enumerate(blocks):
        jb = step * bps + bi
        v2 = jnp.concatenate([v_refs[p][...].reshape(page * heads, HEAD) for p in pages], axis=0)
        ob_ref[jb] = _dot(ps[bi], v2.astype(BF16))
        m_ref[jb] = jnp.broadcast_to(ms[bi], (rows, LANES))
        l_ref[jb] = jnp.broadcast_to(ls[bi], (rows, LANES))
        ksum = k_refs[pages[0]][...].sum(axis=0)
        for p in pages[1:]:
            ksum = ksum + k_refs[p][...].sum(axis=0)
        km_ref[jb] = ksum * (1.0 / MOBA_BLOCK)

    @pl.when(step == pl.num_programs(1) - 1)
    def _():
        tq = lax.broadcasted_iota(jnp.int32, (t_new, LANES), 0)
        kk = lax.broadcasted_iota(jnp.int32, (t_new, LANES), 1)
        causal = jnp.where(kk <= tq, 0.0, NEG)
        zpad = jnp.zeros((LANES - t_new, HEAD), F32)
        m_o, l_o, o_o = [], [], []
        for h in range(heads):
            sl = slice(h * HEAD, (h + 1) * HEAD)
            k_own = jnp.concatenate([kn_ref[:, sl], zpad], axis=0).astype(BF16)
            v_own = jnp.concatenate([vn_ref[:, sl], zpad], axis=0).astype(BF16)
            s_o = _dot_nt(qs[h * t_new:(h + 1) * t_new, :], k_own) + causal
            mh = s_o.max(axis=-1, keepdims=True)
            ph = jnp.exp2(s_o - mh)
            m_o.append(mh)
            l_o.append(ph.sum(axis=-1, keepdims=True))
            o_o.append(_dot(ph.astype(BF16), v_own))
        m_o, l_o, o_o = (jnp.concatenate(x, axis=0) for x in (m_o, l_o, o_o))
        gate = _dot3_nt(q_all, km_ref[...].reshape(nb * heads, HEAD))
        gr = lax.broadcasted_iota(jnp.int32, (rows, nb * heads), 0) // t_new
        gc = jnp.bitwise_and(lax.broadcasted_iota(jnp.int32, (rows, nb * heads), 1), heads - 1)
        sel = _topk_mask(gate, gr == gc, 1)
        picked = [sel[:, jb * heads:(jb + 1) * heads].max(axis=-1, keepdims=True) > 0.5 for jb in range(nb)]
        m_all = m_o
        for jb in range(nb):
            m_all = jnp.maximum(m_all, jnp.where(picked[jb], m_ref[jb][:, 0:1], NEG))
        w_o = jnp.exp2(m_o - m_all)
        l_tot = w_o * l_o
        o_tot = w_o * o_o
        for jb in range(nb):
            wj = jnp.where(picked[jb], jnp.exp2(jnp.where(picked[jb], m_ref[jb][:, 0:1], NEG) - m_all), 0.0)
            l_tot = l_tot + wj * l_ref[jb][:, 0:1]
            o_tot = o_tot + wj * ob_ref[jb]
        res = o_tot / l_tot
        for h in range(heads):
            o_ref[:, h * HEAD:(h + 1) * HEAD] = res[h * t_new:(h + 1) * t_new, :]


def _moba_sample(cache_k, cache_v, layer, page_table, qk, v, seq, t_new, pps):
    depth, n_pool, page, heads, _ = cache_k.shape
    batch, n_pages = page_table.shape
    assert heads & (heads - 1) == 0
    w = heads * HEAD
    nb = n_pages * page // MOBA_BLOCK
    n_steps = n_pages // pps
    r0 = seq // t_new
    rows = heads * t_new

    def page_spec(p):
        return pl.BlockSpec((None, None, page, heads, HEAD),
                            lambda b, t, pt: (layer, pt[b * n_pages + t * pps + p], 0, 0, 0))

    grid_spec = pltpu.PrefetchScalarGridSpec(
        num_scalar_prefetch=1,
        grid=(batch, n_steps),
        in_specs=([page_spec(p) for p in range(pps)] + [page_spec(p) for p in range(pps)] +
                  [pl.BlockSpec((t_new, w), lambda b, t, pt: (r0 + b, 0)),
                   pl.BlockSpec((t_new, w), lambda b, t, pt: (r0 + b, 1)),
                   pl.BlockSpec((t_new, w), lambda b, t, pt: (r0 + b, 0))]),
        out_specs=pl.BlockSpec((t_new, w), lambda b, t, pt: (b, 0)),
        scratch_shapes=[pltpu.VMEM((nb, rows, LANES), F32),
                        pltpu.VMEM((nb, rows, LANES), F32),
                        pltpu.VMEM((nb, rows, HEAD), F32),
                        pltpu.VMEM((nb, heads, HEAD), F32)],
    )
    return pl.pallas_call(
        functools.partial(_moba_sample_kernel, heads=heads, t_new=t_new, pps=pps, nb=nb, page=page),
        grid_spec=grid_spec,
        out_shape=jax.ShapeDtypeStruct((batch * t_new, w), F32),
        compiler_params=_cparams(("parallel", "arbitrary")),
        name="moba_sample",
    )(page_table.reshape(-1), *([cache_k] * pps), *([cache_v] * pps), qk, qk, v)


def _merge_kernel(orp_ref, ors_ref, oap_ref, oas_ref, gr_ref, ga_ref, wr_ref, wa_ref, o_ref, *, n_p):
    yr = _dot(_two_src_load((orp_ref, ors_ref), n_p).astype(BF16), wr_ref[...])
    ya = _dot(_two_src_load((oap_ref, oas_ref), n_p).astype(BF16), wa_ref[...])
    o_ref[...] = (gr_ref[...].astype(F32) * yr + ga_ref[...].astype(F32) * ya).astype(BF16)


def _merge(o_r, o_a, gates, w_r, w_a, layer, bm, seq):
    k = o_r[0].shape[1]
    m = gates.shape[0]
    d = w_r.shape[-1]
    n_p = seq // bm
    return pl.pallas_call(
        functools.partial(_merge_kernel, n_p=n_p),
        grid=(m // bm,),
        in_specs=(_two_src_specs(bm, k, n_p) + _two_src_specs(bm, k, n_p) +
                  [pl.BlockSpec((bm, d), lambda i: (i, 0)),
                   pl.BlockSpec((bm, d), lambda i: (i, 1)),
                   _wspec((layer,), (k, d), lambda i: (0, 0), resident=True),
                   _wspec((layer,), (k, d), lambda i: (0, 0), resident=True)]),
        out_specs=pl.BlockSpec((bm, d), lambda i: (i, 0)),
        out_shape=jax.ShapeDtypeStruct((m, d), BF16),
        compiler_params=_cparams(("parallel",)),
        name="merge",
    )(*o_r, *o_a, gates, gates, w_r, w_a)


def _cross_head(q, k, v):
    s = _dot_nt(q.astype(BF16), k.astype(BF16))
    p = jnp.exp(s - jnp.max(s, axis=-1, keepdims=True))
    l = jnp.sum(p, axis=-1, keepdims=True)
    return _dot(p.astype(BF16), v.astype(BF16)) / l


def _cross_prompt_kernel(q_ref, k_ref, v_ref, o_ref, *, heads):
    for h in range(heads):
        sl = slice(h * HEAD, (h + 1) * HEAD)
        o_ref[:, sl] = _cross_head(q_ref[:, sl], k_ref[:, sl], v_ref[:, sl])


def _cross_sample_kernel(q_ref, k_ref, v_ref, o_ref, *, heads):
    for h in range(heads):
        sl = slice(h * HEAD, (h + 1) * HEAD)
        o_ref[:, sl] = _cross_head(q_ref[:, sl], k_ref[:, h, :], v_ref[:, h, :])


def _cross_prompt(qc, mk, mv, seq, heads, bq):
    w = qc.shape[1]
    n = mk.shape[0]
    return pl.pallas_call(
        functools.partial(_cross_prompt_kernel, heads=heads),
        grid=(seq // bq,),
        in_specs=[pl.BlockSpec((bq, w), lambda i: (i, 0)),
                  pl.BlockSpec((n, w), lambda i: (0, 0)),
                  pl.BlockSpec((n, w), lambda i: (0, 0))],
        out_specs=pl.BlockSpec((bq, w), lambda i: (i, 0)),
        out_shape=jax.ShapeDtypeStruct((seq, w), F32),
        compiler_params=_cparams(("parallel",)),
        name="cross_prompt",
    )(qc, mk, mv)


def _cross_sample(qc, mem_k, mem_v, layer, seq, t_new):
    depth, batch, n, heads, _ = mem_k.shape
    w = heads * HEAD
    r0 = seq // t_new
    mem_spec = pl.BlockSpec((None, None, n, heads, HEAD), lambda b: (layer, b, 0, 0, 0))
    return pl.pallas_call(
        functools.partial(_cross_sample_kernel, heads=heads),
        grid=(batch,),
        in_specs=[pl.BlockSpec((t_new, w), lambda b: (r0 + b, 0)), mem_spec, mem_spec],
        out_specs=pl.BlockSpec((t_new, w), lambda b: (b, 0)),
        out_shape=jax.ShapeDtypeStruct((batch * t_new, w), F32),
        compiler_params=_cparams(("parallel",)),
        name="cross_sample",
    )(qc, mem_k, mem_v)


def _rope_tables(seq, past_len, batch, t_new):
    half = HEAD // 2
    inv_freq = ROPE_THETA ** (-jnp.arange(half, dtype=F32) / half)
    pos = jnp.concatenate([jnp.arange(seq, dtype=jnp.int32),
                           jnp.tile(past_len + jnp.arange(t_new, dtype=jnp.int32), batch)])
    ang = pos.astype(F32)[:, None] * inv_freq[None, :]
    cos = jnp.cos(ang)
    sin = jnp.sin(ang)
    return jnp.concatenate([cos, cos], axis=-1), jnp.concatenate([-sin, sin], axis=-1)


def kernel(x_prompt, x_sample, cache_k, cache_v, state_hgrn, cache_mem_k, cache_mem_v, page_table, mem_prompt,
           norm_pre, norm_post, ffn_gate, ffn_up, ffn_down, w_in, lower_bounds, hgrn_gnorm,
           w_branch_r, w_branch_a, w_out, mem_norm, w_cq, w_ck, w_cv, w_co):
    b_p, seq, d = x_prompt.shape
    batch, t_new, _ = x_sample.shape
    depth, n_pool, page, h_a, _ = cache_k.shape
    h_r = state_hgrn.shape[2]
    h_m = cache_mem_k.shape[3]
    n_mem = cache_mem_k.shape[2]
    n_pages = page_table.shape[1]
    past_len = n_pages * page
    assert b_p == 1 and h_r == 8 and h_a == 8
    assert seq % MOBA_BLOCK == 0 and past_len % MOBA_BLOCK == 0 and t_new <= MOBA_BLOCK
    assert t_new % 8 == 0 and seq % t_new == 0 and MOBA_BLOCK % page == 0
    m_s = batch * t_new
    m = seq + m_s
    w8 = 8 * HEAD

    bm = _pick(m, (768, 512, 384, 256, 128))
    bm_res = _pick(m_s, (256, 128))
    assert seq % bm_res == 0
    bn_ff = _pick(ffn_gate.shape[-1], (512, 256))
    hg_c = _pick(seq, (256,))
    pps = _pick(n_pages, (8, 4, 2))

    cos_t, sin_t = _rope_tables(seq, past_len, batch, t_new)
    lower_bounds = lower_bounds.astype(F32)
    bf = lambda a: a.astype(BF16)
    ffn_down = bf(ffn_down)
    w_branch_r, w_branch_a, w_out = bf(w_branch_r), bf(w_branch_a), bf(w_out)
    w_cq, w_ck, w_cv, w_co = bf(w_cq), bf(w_ck), bf(w_cv), bf(w_co)

    x = (x_prompt.reshape(seq, d), x_sample.reshape(m_s, d))
    u = _norm_cast(x[0], x[1], norm_pre[0, 0][None], bm_res)

    outs = {k: [] for k in ("kp", "vp", "sp", "mkp", "mvp", "ks", "vs", "ss")}
    for l in range(depth):
        g_pre = lambda i: norm_pre[l, i][None]
        g_post = lambda i: norm_post[l, i][None]
        g_next_layer = norm_pre[l + 1, 0][None] if l + 1 < depth else None

        mk_p, mv_p = _memkv(mem_prompt.reshape(n_mem, d), mem_norm[l][None], w_ck, w_cv, l)

        h = _ffn_up(u, ffn_gate, ffn_up, (l, 0), bm, bn_ff)
        x, u = _proj_res(h, ffn_down, (l, 0), x, g_post(0), g_pre(1), 0.5, bm_res, seq)

        pack = _win_hgrn(u, w_in, lower_bounds, l, bm)
        qk = _win_qk(u, w_in, l, cos_t, sin_t, 4, h_a, bm)
        v, vt = _win_v(u, w_in, l, 6, h_a, bm)
        gates = _win_gates(u, w_in, l, 7, 2 * d // w8, bm)

        o_r_p, s_p = _hgrn_prompt(pack, hgrn_gnorm, l, seq, h_r, hg_c)
        o_r_s, s_s = _hgrn_sample(pack, hgrn_gnorm, state_hgrn, l, seq, batch, h_r, t_new)

        o_a_p = _moba_prompt(qk, vt, seq, h_a)
        o_a_s = _moba_sample(cache_k, cache_v, l, page_table, qk, v, seq, t_new, pps)

        merged = _merge((o_r_p, o_r_s), (o_a_p, o_a_s), gates, w_branch_r, w_branch_a, l, bm_res, seq)
        x, u = _proj_res(merged, w_out, (l,), x, g_post(1), g_pre(2), 1.0, bm_res, seq)

        qc = _scaled_proj(u, w_cq, (l,), HEAD ** -0.5, bm)
        oc_p = _cross_prompt(qc, mk_p, mv_p, seq, h_m, _pick(seq, (512, 256)))
        oc_s = _cross_sample(qc, cache_mem_k, cache_mem_v, l, seq, t_new)
        x, u = _proj_res((oc_p, oc_s), w_co, (l,), x, g_post(2), g_pre(3), 1.0, bm_res, seq)

        h = _ffn_up(u, ffn_gate, ffn_up, (l, 1), bm, bn_ff)
        res = _proj_res(h, ffn_down, (l, 1), x, g_post(3), g_next_layer, 0.5, bm_res, seq)
        if g_next_layer is None:
            x = tuple(res)
        else:
            x, u = res

        k_all = qk[:, w8:]
        outs["kp"].append(k_all[:seq].reshape(1, seq, h_a, HEAD))
        outs["vp"].append(v[:seq].reshape(1, seq, h_a, HEAD))
        outs["sp"].append(s_p.reshape(1, h_r, HEAD, HEAD))
        outs["mkp"].append(mk_p.reshape(1, n_mem, h_m, HEAD))
        outs["mvp"].append(mv_p.reshape(1, n_mem, h_m, HEAD))
        outs["ks"].append(k_all[seq:].reshape(batch, t_new, h_a, HEAD))
        outs["vs"].append(v[seq:].reshape(batch, t_new, h_a, HEAD))
        outs["ss"].append(s_s)

    st = lambda k: jnp.stack(outs[k])
    return (x[0].reshape(1, seq, d), x[1].reshape(batch, t_new, d),
            st("kp"), st("vp"), st("sp"), st("mkp"), st("mvp"), st("ks"), st("vs"), st("ss"))
```

```python
import functools

import numpy as np
import jax
import jax.numpy as jnp
from jax import lax
from jax.experimental import pallas as pl
from jax.experimental.pallas import tpu as pltpu

F32 = jnp.float32
BF16 = jnp.bfloat16

MOBA_BLOCK = 256
MOBA_TOPK = 3
ROPE_THETA = 10000.0
EPS = 1e-6
HEAD = 128
LANES = 128
NEG = -1e30
LOG2E = 1.4426950408889634
VMEM_LIMIT = 56 * 1024 * 1024
HGRN_HEADS_PER_STEP = 8


def _cparams(sem, vmem=VMEM_LIMIT):
    return pltpu.CompilerParams(dimension_semantics=sem, vmem_limit_bytes=vmem)


def _pick(n, cands):
    for c in cands:
        if n % c == 0:
            return c
    raise ValueError(f"no tile in {cands} divides {n}")


def _wspec(lead, shape, index_map, resident=False):
    full_shape = (None,) * len(lead) + tuple(shape)
    full_map = lambda *a: tuple(lead) + tuple(index_map(*a))
    if resident:
        return pl.BlockSpec(full_shape, full_map, pipeline_mode=pl.Buffered(1))
    return pl.BlockSpec(full_shape, full_map)


def _two_src_specs(bm, k, n_p):
    return [pl.BlockSpec((bm, k), lambda i: (jnp.minimum(i, n_p - 1), 0)),
            pl.BlockSpec((bm, k), lambda i: (jnp.maximum(i - n_p, 0), 0))]


def _two_src_load(refs, n_p):
    return jnp.where(pl.program_id(0) < n_p, refs[0][...], refs[1][...])


def _dot(a, b):
    return jnp.dot(a, b, preferred_element_type=F32)


def _dot_nt(a, b):
    return lax.dot_general(a, b, (((1,), (1,)), ((), ())), preferred_element_type=F32)


def _split2(a):
    hi = a.astype(BF16)
    lo = (a - hi.astype(F32)).astype(BF16)
    return hi, lo


def _dot3_nt(a, b):
    ah, al = _split2(a)
    bh, bl = _split2(b)
    return _dot_nt(ah, bh) + _dot_nt(ah, bl) + _dot_nt(al, bh)


def _dot3(a, b):
    ah, al = _split2(a)
    bh, bl = _split2(b)
    return _dot(ah, bh) + _dot(ah, bl) + _dot(al, bh)


def _sigmoid(x):
    return 1.0 / (1.0 + jnp.exp(-x))


def _silu(x):
    return x * _sigmoid(x)


def _rms(y, g):
    ms = jnp.mean(y * y, axis=-1, keepdims=True)
    return y * lax.rsqrt(ms + EPS) * g


def _cast_once(w_ref, wb_ref, axis):
    @pl.when(pl.program_id(axis) == 0)
    def _():
        wb_ref[...] = w_ref[...].astype(BF16)


def _norm_kernel(xp_ref, xs_ref, g_ref, u_ref, *, n_p):
    u_ref[...] = _rms(_two_src_load((xp_ref, xs_ref), n_p), g_ref[...]).astype(BF16)


def _norm_cast(x_p, x_s, g, bm):
    seq, d = x_p.shape
    m = seq + x_s.shape[0]
    n_p = seq // bm
    return pl.pallas_call(
        functools.partial(_norm_kernel, n_p=n_p),
        grid=(m // bm,),
        in_specs=_two_src_specs(bm, d, n_p) + [pl.BlockSpec((1, d), lambda i: (0, 0))],
        out_specs=pl.BlockSpec((bm, d), lambda i: (i, 0)),
        out_shape=jax.ShapeDtypeStruct((m, d), BF16),
        compiler_params=_cparams(("parallel",)),
        name="norm_cast",
    )(x_p, x_s, g)


def _ffn_up_kernel(u_ref, wg_ref, wu_ref, h_ref, wgb_ref, wub_ref):
    _cast_once(wg_ref, wgb_ref, 1)
    _cast_once(wu_ref, wub_ref, 1)
    u = u_ref[...]
    h_ref[...] = (_silu(_dot(u, wgb_ref[...])) * _dot(u, wub_ref[...])).astype(BF16)


def _ffn_up(u, wg, wu, lead, bm, bn):
    m, d = u.shape
    f = wg.shape[-1]
    return pl.pallas_call(
        _ffn_up_kernel,
        grid=(f // bn, m // bm),
        in_specs=[pl.BlockSpec((bm, d), lambda n, i: (i, 0)),
                  _wspec(lead, (d, bn), lambda n, i: (0, n)),
                  _wspec(lead, (d, bn), lambda n, i: (0, n))],
        out_specs=pl.BlockSpec((bm, bn), lambda n, i: (i, n)),
        out_shape=jax.ShapeDtypeStruct((m, f), BF16),
        scratch_shapes=[pltpu.VMEM((d, bn), BF16), pltpu.VMEM((d, bn), BF16)],
        compiler_params=_cparams(("arbitrary", "arbitrary")),
        name="ffn_up",
    )(u, wg, wu)


def _proj_res_kernel(*refs, scale, a2, x2, split, n_p):
    refs = list(refs)
    a_refs = [refs.pop(0) for _ in range(2 if a2 else 1)]
    w_ref = refs.pop(0)
    x_refs = [refs.pop(0) for _ in range(2 if x2 else 1)]
    gp_ref = refs.pop(0)
    a = _two_src_load(a_refs, n_p) if a2 else a_refs[0][...]
    x = _two_src_load(x_refs, n_p) if x2 else x_refs[0][...]
    xn = x + scale * _rms(_dot(a.astype(BF16), w_ref[...]), gp_ref[...])
    if split:
        xop_ref, xos_ref = refs
        is_p = pl.program_id(0) < n_p

        @pl.when(is_p)
        def _():
            xop_ref[...] = xn

        @pl.when(jnp.logical_not(is_p))
        def _():
            xos_ref[...] = xn
    else:
        gn_ref, xo_ref, uo_ref = refs
        xo_ref[...] = xn
        uo_ref[...] = _rms(xn, gn_ref[...]).astype(BF16)


def _proj_res(a, w, lead, x, g_post, g_next, scale, bm, seq):
    a2, x2, split = isinstance(a, tuple), isinstance(x, tuple), g_next is None
    k, d = w.shape[-2:]
    n_p = seq // bm
    m = (x[0].shape[0] + x[1].shape[0]) if x2 else x.shape[0]
    row = lambda c: pl.BlockSpec((bm, c), lambda i: (i, 0))
    vec = pl.BlockSpec((1, d), lambda i: (0, 0))
    in_specs = (_two_src_specs(bm, k, n_p) if a2 else [row(k)])
    in_specs += [_wspec(lead, (k, d), lambda i: (0, 0), resident=True)]
    in_specs += (_two_src_specs(bm, d, n_p) if x2 else [row(d)]) + [vec]
    args = (list(a) if a2 else [a]) + [w] + (list(x) if x2 else [x]) + [g_post]
    if split:
        out_specs = _two_src_specs(bm, d, n_p)
        out_shape = [jax.ShapeDtypeStruct((seq, d), F32), jax.ShapeDtypeStruct((m - seq, d), F32)]
    else:
        in_specs += [vec]
        args += [g_next]
        out_specs = [row(d), row(d)]
        out_shape = [jax.ShapeDtypeStruct((m, d), F32), jax.ShapeDtypeStruct((m, d), BF16)]
    return pl.pallas_call(
        functools.partial(_proj_res_kernel, scale=scale, a2=a2, x2=x2, split=split, n_p=n_p),
        grid=(m // bm,),
        in_specs=in_specs,
        out_specs=out_specs,
        out_shape=out_shape,
        compiler_params=_cparams(("arbitrary",)),
        name="proj_res",
    )(*args)


def _win_hgrn_kernel(u_ref, w_ref, lb_ref, o_ref, wb_ref, *, layer):
    _cast_once(w_ref, wb_ref, 1)
    n = pl.program_id(0)
    acc = _dot(u_ref[...], wb_ref[...])

    @pl.when(n == 0)
    def _():
        o_ref[...] = _silu(acc) * (HEAD ** -0.5)

    @pl.when(n == 1)
    def _():
        lbs = lb_ref[...]
        e = jnp.exp(lbs - jnp.max(lbs, axis=0, keepdims=True))
        sm = e / jnp.sum(e, axis=0, keepdims=True)
        cum = sm[0:1, :]
        for i in range(1, layer + 1):
            cum = cum + sm[i:i + 1, :]
        lb = cum - sm[0:1, :]
        ls = jnp.minimum(acc, 0.0) - jnp.log1p(jnp.exp(-jnp.abs(acc)))
        a = jnp.log(lb)
        b = jnp.log1p(-lb) + ls
        o_ref[...] = jnp.maximum(a, b) + jnp.log1p(jnp.exp(-jnp.abs(a - b)))

    @pl.when(n == 2)
    def _():
        o_ref[...] = acc

    @pl.when(n == 3)
    def _():
        o_ref[...] = _silu(acc)


def _win_hgrn(u, w_in, lower_bounds, layer, bm):
    m, d = u.shape
    w = 8 * HEAD
    depth = lower_bounds.shape[0]
    return pl.pallas_call(
        functools.partial(_win_hgrn_kernel, layer=layer),
        grid=(4, m // bm),
        in_specs=[pl.BlockSpec((bm, d), lambda n, i: (i, 0)),
                  _wspec((layer,), (d, w), lambda n, i: (0, n)),
                  pl.BlockSpec((depth, w), lambda n, i: (0, 0))],
        out_specs=pl.BlockSpec((bm, w), lambda n, i: (i, n)),
        out_shape=jax.ShapeDtypeStruct((m, 4 * w), F32),
        scratch_shapes=[pltpu.VMEM((d, w), BF16)],
        compiler_params=_cparams(("arbitrary", "arbitrary")),
        name="win_hgrn",
    )(u, w_in, lower_bounds)


def _win_qk_kernel(u_ref, w_ref, cos_ref, sin_ref, o_ref, kb_ref, wb_ref, *, heads):
    _cast_once(w_ref, wb_ref, 1)
    acc = _dot(u_ref[...], wb_ref[...])
    c = cos_ref[...]
    s = sin_ref[...]
    for h in range(heads):
        sl = slice(h * HEAD, (h + 1) * HEAD)
        xh = acc[:, sl]
        r = xh * c + pltpu.roll(xh, HEAD // 2, 1) * s
        o_ref[:, sl] = r
        kb_ref[:, sl] = r.astype(BF16)


def _win_qk(u, w_in, layer, cos_t, sin_t, col_blk, heads, bm):
    m, d = u.shape
    w = heads * HEAD
    return pl.pallas_call(
        functools.partial(_win_qk_kernel, heads=heads),
        grid=(2, m // bm),
        in_specs=[pl.BlockSpec((bm, d), lambda n, i: (i, 0)),
                  _wspec((layer,), (d, w), lambda n, i: (0, col_blk + n)),
                  pl.BlockSpec((bm, HEAD), lambda n, i: (i, 0)),
                  pl.BlockSpec((bm, HEAD), lambda n, i: (i, 0))],
        out_specs=[pl.BlockSpec((bm, w), lambda n, i: (i, n)),
                   pl.BlockSpec((bm, w), lambda n, i: (i, n))],
        out_shape=[jax.ShapeDtypeStruct((m, 2 * w), F32), jax.ShapeDtypeStruct((m, 2 * w), BF16)],
        scratch_shapes=[pltpu.VMEM((d, w), BF16)],
        compiler_params=_cparams(("arbitrary", "arbitrary")),
        name="win_qk",
    )(u, w_in, cos_t, sin_t)


def _win_v_kernel(u_ref, w_ref, v_ref, vt_ref, wb_ref):
    _cast_once(w_ref, wb_ref, 0)
    acc = _dot(u_ref[...], wb_ref[...])
    v_ref[...] = acc
    vt_ref[...] = acc.T.astype(BF16)


def _win_v(u, w_in, layer, col_blk, heads, bm):
    m, d = u.shape
    w = heads * HEAD
    return pl.pallas_call(
        _win_v_kernel,
        grid=(m // bm,),
        in_specs=[pl.BlockSpec((bm, d), lambda i: (i, 0)),
                  _wspec((layer,), (d, w), lambda i: (0, col_blk), resident=True)],
        out_specs=[pl.BlockSpec((bm, w), lambda i: (i, 0)), pl.BlockSpec((w, bm), lambda i: (0, i))],
        out_shape=[jax.ShapeDtypeStruct((m, w), F32), jax.ShapeDtypeStruct((w, m), BF16)],
        scratch_shapes=[pltpu.VMEM((d, w), BF16)],
        compiler_params=_cparams(("arbitrary",)),
        name="win_v",
    )(u, w_in)


def _win_gate_kernel(u_ref, w_ref, o_ref, wb_ref):
    _cast_once(w_ref, wb_ref, 1)
    o_ref[...] = _sigmoid(_dot(u_ref[...], wb_ref[...])).astype(BF16)


def _win_gates(u, w_in, layer, col_blk, n_blk, bm):
    m, d = u.shape
    w = 8 * HEAD
    return pl.pallas_call(
        _win_gate_kernel,
        grid=(n_blk, m // bm),
        in_specs=[pl.BlockSpec((bm, d), lambda n, i: (i, 0)),
                  _wspec((layer,), (d, w), lambda n, i: (0, col_blk + n))],
        out_specs=pl.BlockSpec((bm, w), lambda n, i: (i, n)),
        out_shape=jax.ShapeDtypeStruct((m, n_blk * w), BF16),
        scratch_shapes=[pltpu.VMEM((d, w), BF16)],
        compiler_params=_cparams(("arbitrary", "arbitrary")),
        name="win_gates",
    )(u, w_in)


def _scaled_proj_kernel(u_ref, w_ref, o_ref, *, scale):
    o_ref[...] = _dot(u_ref[...], w_ref[...]) * scale


def _scaled_proj(u, w, lead, scale, bm):
    m, d = u.shape
    n = w.shape[-1]
    return pl.pallas_call(
        functools.partial(_scaled_proj_kernel, scale=scale),
        grid=(m // bm,),
        in_specs=[pl.BlockSpec((bm, d), lambda i: (i, 0)),
                  _wspec(lead, (d, n), lambda i: (0, 0), resident=True)],
        out_specs=pl.BlockSpec((bm, n), lambda i: (i, 0)),
        out_shape=jax.ShapeDtypeStruct((m, n), F32),
        compiler_params=_cparams(("parallel",)),
        name="scaled_proj",
    )(u, w)


def _memkv_kernel(mem_ref, g_ref, wk_ref, wv_ref, k_ref, v_ref):
    mn = _rms(mem_ref[...], g_ref[...]).astype(BF16)
    k_ref[...] = _dot(mn, wk_ref[...])
    v_ref[...] = _dot(mn, wv_ref[...])


def _memkv(mem, g, wk, wv, layer):
    n, d = mem.shape
    w = wk.shape[-1]
    return pl.pallas_call(
        _memkv_kernel,
        grid=(1,),
        in_specs=[pl.BlockSpec((n, d), lambda i: (0, 0)),
                  pl.BlockSpec((1, d), lambda i: (0, 0)),
                  _wspec((layer,), (d, w), lambda i: (0, 0)),
                  _wspec((layer,), (d, w), lambda i: (0, 0))],
        out_specs=[pl.BlockSpec((n, w), lambda i: (0, 0)), pl.BlockSpec((n, w), lambda i: (0, 0))],
        out_shape=[jax.ShapeDtypeStruct((n, w), F32), jax.ShapeDtypeStruct((n, w), F32)],
        compiler_params=_cparams(("arbitrary",)),
        name="memkv",
    )(mem, g, wk, wv)


def _cumsum_rows(g, c):
    row = lax.broadcasted_iota(jnp.int32, g.shape, 0)
    out = g
    sh = 1
    while sh < c:
        out = out + jnp.where(row >= sh, pltpu.roll(out, sh, 0), 0.0)
        sh *= 2
    return out


def _hgrn_chunk(q, g, v, st, c):
    g = g * LOG2E
    k = 1.0 - jnp.exp2(g)
    gc = _cumsum_rows(g, c)
    row = lax.broadcasted_iota(jnp.int32, (c, HEAD), 0)
    ra = lax.broadcasted_iota(jnp.int32, (c, c), 0)
    ca = lax.broadcasted_iota(jnp.int32, (c, c), 1)
    x = jnp.bitwise_xor(ra, ca)

    a = jnp.zeros((c, c), F32)
    b = c // 2
    while b >= 1:
        upper = jnp.bitwise_and(row, b) != 0
        if b >= 4:
            nb = c // (2 * b)
            gcb = gc.reshape(nb, 2 * b, HEAD)
            r = gcb[:, b - 1:b, :]
            up3 = upper.reshape(nb, 2 * b, HEAD)
            dlt = jnp.where(up3, gcb - r, r - gcb).reshape(c, HEAD)
        elif b == 2:
            t4 = jnp.bitwise_and(row, 3)
            g_next = pltpu.roll(g, c - 1, 0)
            g_prev = pltpu.roll(g, 1, 0)
            dlt = jnp.where(t4 == 0, g_next, jnp.where(t4 == 1, 0.0, jnp.where(t4 == 2, g, g + g_prev)))
        else:
            dlt = jnp.where(upper, g, 0.0)
        z = (jnp.where(upper, q, k) * jnp.exp2(dlt)).astype(BF16)
        p = _dot_nt(z, z)
        a = p if b == c // 2 else jnp.where(x < 2 * b, p, a)
        b //= 2
    a = jnp.where(ra > ca, a, 0.0)

    vb = v.astype(BF16)
    o = _dot(a.astype(BF16), vb)
    o = o + jnp.sum(q * k, axis=-1, keepdims=True) * v
    o = o + _dot_nt((q * jnp.exp2(gc)).astype(BF16), st.astype(BF16))
    g_end = gc[c - 1:c, :]
    khat = (k * jnp.exp2(g_end - gc)).astype(BF16)
    st_new = st * jnp.exp2(g_end) + lax.dot_general(vb, khat, (((0,), (0,)), ((), ())),
                                                    preferred_element_type=F32)
    return o, st_new


def _hgrn_out(o, gn, og):
    return _rms(o, gn) * og


def _hgrn_prompt_kernel(q_ref, g_ref, v_ref, og_ref, gn_ref, o_ref, s_ref, st_ref, *, c, hpb):
    hg = pl.program_id(0)
    ci = pl.program_id(1)

    @pl.when(ci == 0)
    def _():
        st_ref[...] = jnp.zeros_like(st_ref)

    gn_all = gn_ref[...]
    hrow = lax.broadcasted_iota(jnp.int32, gn_all.shape, 0)
    sls = [slice(j * HEAD, (j + 1) * HEAD) for j in range(hpb)]
    res = [_hgrn_chunk(q_ref[:, sl], g_ref[:, sl], v_ref[:, sl], st_ref[j], c) for j, sl in enumerate(sls)]
    for j, (o, st_new) in enumerate(res):
        gn = jnp.sum(jnp.where(hrow == hg * hpb + j, gn_all, 0.0), axis=0, keepdims=True)
        st_ref[j] = st_new
        o_ref[:, sls[j]] = _hgrn_out(o, gn, og_ref[:, sls[j]])

    @pl.when(ci == pl.num_programs(1) - 1)
    def _():
        for j, (_, st_new) in enumerate(res):
            s_ref[j] = st_new.T


def _hgrn_prompt(pack, gnorm, layer, seq, heads, c, hpb):
    n_hg = heads // hpb
    blk = lambda off: pl.BlockSpec((c, hpb * HEAD), lambda h, i: (i, off * n_hg + h))
    return pl.pallas_call(
        functools.partial(_hgrn_prompt_kernel, c=c, hpb=hpb),
        grid=(n_hg, seq // c),
        in_specs=[blk(0), blk(1), blk(2), blk(3),
                  _wspec((layer,), (heads, HEAD), lambda h, i: (0, 0))],
        out_specs=[pl.BlockSpec((c, hpb * HEAD), lambda h, i: (i, h)),
                   pl.BlockSpec((hpb, HEAD, HEAD), lambda h, i: (h, 0, 0))],
        out_shape=[jax.ShapeDtypeStruct((seq, heads * HEAD), F32),
                   jax.ShapeDtypeStruct((heads, HEAD, HEAD), F32)],
        scratch_shapes=[pltpu.VMEM((hpb, HEAD, HEAD), F32)],
        compiler_params=_cparams(("parallel", "arbitrary")),
        name="hgrn_prompt",
    )(pack, pack, pack, pack, gnorm)


def _hgrn_sample_kernel(q_ref, g_ref, v_ref, og_ref, gn_ref, s0_ref, o_ref, s_ref, *, c, heads):
    for h in range(heads):
        sl = slice(h * HEAD, (h + 1) * HEAD)
        o, st_new = _hgrn_chunk(q_ref[:, sl], g_ref[:, sl], v_ref[:, sl], s0_ref[h].T, c)
        o_ref[:, sl] = _hgrn_out(o, gn_ref[h:h + 1, :], og_ref[:, sl])
        s_ref[h] = st_new.T


def _hgrn_sample(pack, gnorm, s0, layer, seq, batch, heads, c):
    w = heads * HEAD
    r0 = seq // c
    blk = lambda off: pl.BlockSpec((c, w), lambda b: (r0 + b, off))
    return pl.pallas_call(
        functools.partial(_hgrn_sample_kernel, c=c, heads=heads),
        grid=(batch,),
        in_specs=[blk(0), blk(1), blk(2), blk(3),
                  _wspec((layer,), (heads, HEAD), lambda b: (0, 0)),
                  _wspec((layer,), (None, heads, HEAD, HEAD), lambda b: (b, 0, 0, 0))],
        out_specs=[pl.BlockSpec((c, w), lambda b: (b, 0)),
                   pl.BlockSpec((None, heads, HEAD, HEAD), lambda b: (b, 0, 0, 0))],
        out_shape=[jax.ShapeDtypeStruct((batch * c, w), F32),
                   jax.ShapeDtypeStruct(s0.shape[1:], F32)],
        compiler_params=_cparams(("parallel",)),
        name="hgrn_sample",
    )(pack, pack, pack, pack, gnorm, s0)


def _topk_mask(score, valid, axis):
    idx = lax.broadcasted_iota(jnp.int32, score.shape, axis)
    big = jnp.int32(2 ** 30)
    sg = jnp.where(valid, score, -jnp.inf)
    sel = jnp.zeros(score.shape, F32)
    for _ in range(MOBA_TOPK):
        mx = jnp.max(sg, axis=axis, keepdims=True)
        first = jnp.min(jnp.where(sg == mx, idx, big), axis=axis, keepdims=True)
        pick = jnp.logical_and(idx == first, mx > -jnp.inf)
        sel = jnp.where(pick, 1.0, sel)
        sg = jnp.where(pick, -jnp.inf, sg)
    return sel


def _moba_prompt_kernel(it_ref, jt_ref, fl_ref, q_ref, ko_ref, vto_ref, kp_ref, vtp_ref, o_ref,
                        qs_ref, m_ref, l_ref, acc_ref, sel_ref, km_ref, *, heads, nbp):
    s = pl.program_id(0)
    i = it_ref[s]
    jp = jt_ref[s]
    own = jnp.bitwise_and(fl_ref[s], 1) == 1
    last = jnp.bitwise_and(fl_ref[s], 2) == 2
    blk = MOBA_BLOCK
    hs = [slice(h * HEAD, (h + 1) * HEAD) for h in range(heads)]

    @pl.when(s == 0)
    def _():
        km_ref[...] = jnp.zeros_like(km_ref)

    @pl.when(own)
    def _():
        k = ko_ref[...]
        kb = k.astype(BF16)
        km_ref[pl.ds(i, 1), :] = jnp.mean(k, axis=0, keepdims=True)
        qt = q_ref[...].T
        qs_ref[...] = (qt * (HEAD ** -0.5 * LOG2E)).astype(BF16)
        key = lax.broadcasted_iota(jnp.int32, (blk, blk), 0)
        qry = lax.broadcasted_iota(jnp.int32, (blk, blk), 1)
        causal = jnp.where(key <= qry, 0.0, NEG)
        bidx = lax.broadcasted_iota(jnp.int32, (nbp, blk), 0)
        km = km_ref[...]
        qsb = qs_ref[...]
        gates = [_dot3(km[:, sl], qt[sl, :]) for sl in hs]
        sts = [_dot(kb[:, sl], qsb[sl, :]) for sl in hs]
        m_rows, l_rows, ps = [], [], []
        for h in range(heads):
            st = sts[h] + causal
            m = jnp.max(st, axis=0, keepdims=True)
            p = jnp.exp2(st - m)
            m_rows.append(m)
            l_rows.append(jnp.sum(p, axis=0, keepdims=True))
            ps.append(p.astype(BF16))
        for h, sl in enumerate(hs):
            acc_ref[sl, :] = _dot(vto_ref[sl, :], ps[h])
            sel_ref[h] = _topk_mask(gates[h], bidx < i, 0)
        m_ref[...] = jnp.concatenate(m_rows, axis=0)
        l_ref[...] = jnp.concatenate(l_rows, axis=0)

    @pl.when(jnp.logical_not(own))
    def _():
        kb = kp_ref[...]
        m_all = m_ref[...]
        l_all = l_ref[...]
        sel_a = sel_ref[:, pl.ds(2 * jp, 1), :]
        sel_b = sel_ref[:, pl.ds(2 * jp + 1, 1), :]
        sts = [_dot(kb[:, sl], qs_ref[sl, :]) for sl in hs]
        m_rows, l_rows, ps, alphas = [], [], [], []
        for h in range(heads):
            st_a = sts[h][:blk] + (sel_a[h] - 1.0) * (-NEG)
            st_b = sts[h][blk:] + (sel_b[h] - 1.0) * (-NEG)
            m_old = m_all[h:h + 1, :]
            m_new = jnp.maximum(m_old, jnp.maximum(jnp.max(st_a, axis=0, keepdims=True),
                                                   jnp.max(st_b, axis=0, keepdims=True)))
            alpha = jnp.exp2(m_old - m_new)
            p_a = jnp.exp2(st_a - m_new)
            p_b = jnp.exp2(st_b - m_new)
            m_rows.append(m_new)
            l_rows.append(alpha * l_all[h:h + 1, :] + jnp.sum(p_a, axis=0, keepdims=True)
                          + jnp.sum(p_b, axis=0, keepdims=True))
            ps.append(jnp.concatenate([p_a.astype(BF16), p_b.astype(BF16)], axis=0))
            alphas.append(alpha)
        for h, sl in enumerate(hs):
            acc_ref[sl, :] = alphas[h] * acc_ref[sl, :] + _dot(vtp_ref[sl, :], ps[h])
        m_ref[...] = jnp.concatenate(m_rows, axis=0)
        l_ref[...] = jnp.concatenate(l_rows, axis=0)

    @pl.when(last)
    def _():
        for h, sl in enumerate(hs):
            o_ref[:, sl] = (acc_ref[sl, :] / l_ref[h:h + 1, :]).T


def _moba_prompt(qk, qk_bf, vt, seq, heads):
    w = heads * HEAD
    blk = MOBA_BLOCK
    nb = seq // blk
    assert nb % 2 == 0
    nbp = -(-nb // 8) * 8
    it, jt, fl = [], [], []
    for i in range(nb):
        it.append(i)
        jt.append(0)
        fl.append(1)
        for jp in range((i + 1) // 2):
            it.append(i)
            jt.append(jp)
            fl.append(0)
        fl[-1] += 2
    tabs = [jnp.asarray(np.asarray(t, np.int32)) for t in (it, jt, fl)]
    grid_spec = pltpu.PrefetchScalarGridSpec(
        num_scalar_prefetch=3,
        grid=(len(it),),
        in_specs=[pl.BlockSpec((blk, w), lambda s, it, jt, fl: (it[s], 0)),
                  pl.BlockSpec((blk, w), lambda s, it, jt, fl: (it[s], 1)),
                  pl.BlockSpec((w, blk), lambda s, it, jt, fl: (0, it[s])),
                  pl.BlockSpec((2 * blk, w), lambda s, it, jt, fl: (jt[s], 1)),
                  pl.BlockSpec((w, 2 * blk), lambda s, it, jt, fl: (0, jt[s]))],
        out_specs=pl.BlockSpec((blk, w), lambda s, it, jt, fl: (it[s], 0)),
        scratch_shapes=[pltpu.VMEM((w, blk), BF16),
                        pltpu.VMEM((heads, blk), F32),
                        pltpu.VMEM((heads, blk), F32),
                        pltpu.VMEM((w, blk), F32),
                        pltpu.VMEM((heads, nbp, blk), F32),
                        pltpu.VMEM((nbp, w), F32)],
    )
    return pl.pallas_call(
        functools.partial(_moba_prompt_kernel, heads=heads, nbp=nbp),
        grid_spec=grid_spec,
        out_shape=jax.ShapeDtypeStruct((seq, w), F32),
        compiler_params=_cparams(("arbitrary",)),
        name="moba_prompt",
    )(*tabs, qk, qk, vt, qk_bf, vt)


def _moba_sample_kernel(pt_ref, *refs, heads, t_new, pps, nb, page):
    del pt_ref
    k_refs = refs[:pps]
    v_refs = refs[pps:2 * pps]
    q_ref, kn_ref, vn_ref, o_ref, m_ref, l_ref, ob_ref, km_ref = refs[2 * pps:]
    step = pl.program_id(1)
    ppb = MOBA_BLOCK // page
    bps = pps // ppb
    rows = heads * t_new
    cols = ppb * page * heads

    q = q_ref[...]
    q_all = jnp.concatenate([q[:, h * HEAD:(h + 1) * HEAD] for h in range(heads)], axis=0)
    qs = (q_all * (HEAD ** -0.5 * LOG2E)).astype(BF16)

    rh = lax.broadcasted_iota(jnp.int32, (rows, cols), 0) // t_new
    ch = jnp.bitwise_and(lax.broadcasted_iota(jnp.int32, (rows, cols), 1), heads - 1)
    head_bias = jnp.where(rh == ch, 0.0, NEG)

    blocks = [range(bi * ppb, (bi + 1) * ppb) for bi in range(bps)]
    scs = []
    for pages in blocks:
        k2 = jnp.concatenate([k_refs[p][...].reshape(page * heads, HEAD) for p in pages], axis=0)
        scs.append(_dot_nt(qs, k2.astype(BF16)))
    ms, ls, ps = [], [], []
    for sc in scs:
        sc = sc + head_bias
        m = sc.max(axis=-1, keepdims=True)
        p_exp = jnp.exp2(sc - m)
        ms.append(m)
        ls.append(p_exp.sum(axis=-1, keepdims=True))
        ps.append(p_exp.astype(BF16))
    for bi, pages in enumerate(blocks):
        jb = step * bps + bi
        v2 = jnp.concatenate([v_refs[p][...].reshape(page * heads, HEAD) for p in pages], axis=0)
        ob_ref[jb] = _dot(ps[bi], v2.astype(BF16))
        m_ref[jb] = jnp.broadcast_to(ms[bi], (rows, LANES))
        l_ref[jb] = jnp.broadcast_to(ls[bi], (rows, LANES))
        ksum = k_refs[pages[0]][...].sum(axis=0)
        for p in pages[1:]:
            ksum = ksum + k_refs[p][...].sum(axis=0)
        km_ref[jb] = ksum * (1.0 / MOBA_BLOCK)

    @pl.when(step == pl.num_programs(1) - 1)
    def _():
        tq = lax.broadcasted_iota(jnp.int32, (t_new, LANES), 0)
        kk = lax.broadcasted_iota(jnp.int32, (t_new, LANES), 1)
        causal = jnp.where(kk <= tq, 0.0, NEG)
        zpad = jnp.zeros((LANES - t_new, HEAD), F32)
        m_o, l_o, o_o = [], [], []
        for h in range(heads):
            sl = slice(h * HEAD, (h + 1) * HEAD)
            k_own = jnp.concatenate([kn_ref[:, sl], zpad], axis=0).astype(BF16)
            v_own = jnp.concatenate([vn_ref[:, sl], zpad], axis=0).astype(BF16)
            s_o = _dot_nt(qs[h * t_new:(h + 1) * t_new, :], k_own) + causal
            mh = s_o.max(axis=-1, keepdims=True)
            ph = jnp.exp2(s_o - mh)
            m_o.append(mh)
            l_o.append(ph.sum(axis=-1, keepdims=True))
            o_o.append(_dot(ph.astype(BF16), v_own))
        m_o, l_o, o_o = (jnp.concatenate(x, axis=0) for x in (m_o, l_o, o_o))
        gate = _dot3_nt(q_all, km_ref[...].reshape(nb * heads, HEAD))
        gr = lax.broadcasted_iota(jnp.int32, (rows, nb * heads), 0) // t_new
        gc = jnp.bitwise_and(lax.broadcasted_iota(jnp.int32, (rows, nb * heads), 1), heads - 1)
        sel = _topk_mask(gate, gr == gc, 1)
        picked = [sel[:, jb * heads:(jb + 1) * heads].max(axis=-1, keepdims=True) > 0.5 for jb in range(nb)]
        m_all = m_o
        for jb in range(nb):
            m_all = jnp.maximum(m_all, jnp.where(picked[jb], m_ref[jb][:, 0:1], NEG))
        w_o = jnp.exp2(m_o - m_all)
        l_tot = w_o * l_o
        o_tot = w_o * o_o
        for jb in range(nb):
            wj = jnp.where(picked[jb], jnp.exp2(jnp.where(picked[jb], m_ref[jb][:, 0:1], NEG) - m_all), 0.0)
            l_tot = l_tot + wj * l_ref[jb][:, 0:1]
            o_tot = o_tot + wj * ob_ref[jb]
        res = o_tot / l_tot
        for h in range(heads):
            o_ref[:, h * HEAD:(h + 1) * HEAD] = res[h * t_new:(h + 1) * t_new, :]


def _moba_sample(cache_k, cache_v, layer, page_table, qk, v, seq, t_new, pps):
    depth, n_pool, page, heads, _ = cache_k.shape
    batch, n_pages = page_table.shape
    assert heads & (heads - 1) == 0
    w = heads * HEAD
    nb = n_pages * page // MOBA_BLOCK
    n_steps = n_pages // pps
    r0 = seq // t_new
    rows = heads * t_new

    def page_spec(p):
        return pl.BlockSpec((None, None, page, heads, HEAD),
                            lambda b, t, pt: (layer, pt[b * n_pages + t * pps + p], 0, 0, 0))

    grid_spec = pltpu.PrefetchScalarGridSpec(
        num_scalar_prefetch=1,
        grid=(batch, n_steps),
        in_specs=([page_spec(p) for p in range(pps)] + [page_spec(p) for p in range(pps)] +
                  [pl.BlockSpec((t_new, w), lambda b, t, pt: (r0 + b, 0)),
                   pl.BlockSpec((t_new, w), lambda b, t, pt: (r0 + b, 1)),
                   pl.BlockSpec((t_new, w), lambda b, t, pt: (r0 + b, 0))]),
        out_specs=pl.BlockSpec((t_new, w), lambda b, t, pt: (b, 0)),
        scratch_shapes=[pltpu.VMEM((nb, rows, LANES), F32),
                        pltpu.VMEM((nb, rows, LANES), F32),
                        pltpu.VMEM((nb, rows, HEAD), F32),
                        pltpu.VMEM((nb, heads, HEAD), F32)],
    )
    return pl.pallas_call(
        functools.partial(_moba_sample_kernel, heads=heads, t_new=t_new, pps=pps, nb=nb, page=page),
        grid_spec=grid_spec,
        out_shape=jax.ShapeDtypeStruct((batch * t_new, w), F32),
        compiler_params=_cparams(("parallel", "arbitrary")),
        name="moba_sample",
    )(page_table.reshape(-1), *([cache_k] * pps), *([cache_v] * pps), qk, qk, v)


def _merge_kernel(orp_ref, ors_ref, oap_ref, oas_ref, gr_ref, ga_ref, wr_ref, wa_ref, o_ref, *, n_p):
    yr = _dot(_two_src_load((orp_ref, ors_ref), n_p).astype(BF16), wr_ref[...])
    ya = _dot(_two_src_load((oap_ref, oas_ref), n_p).astype(BF16), wa_ref[...])
    o_ref[...] = (gr_ref[...].astype(F32) * yr + ga_ref[...].astype(F32) * ya).astype(BF16)


def _merge(o_r, o_a, gates, w_r, w_a, layer, bm, seq):
    k = o_r[0].shape[1]
    m = gates.shape[0]
    d = w_r.shape[-1]
    n_p = seq // bm
    return pl.pallas_call(
        functools.partial(_merge_kernel, n_p=n_p),
        grid=(m // bm,),
        in_specs=(_two_src_specs(bm, k, n_p) + _two_src_specs(bm, k, n_p) +
                  [pl.BlockSpec((bm, d), lambda i: (i, 0)),
                   pl.BlockSpec((bm, d), lambda i: (i, 1)),
                   _wspec((layer,), (k, d), lambda i: (0, 0), resident=True),
                   _wspec((layer,), (k, d), lambda i: (0, 0), resident=True)]),
        out_specs=pl.BlockSpec((bm, d), lambda i: (i, 0)),
        out_shape=jax.ShapeDtypeStruct((m, d), BF16),
        compiler_params=_cparams(("parallel",)),
        name="merge",
    )(*o_r, *o_a, gates, gates, w_r, w_a)


def _cross_head(q, k, v):
    s = _dot_nt(q.astype(BF16), k.astype(BF16))
    p = jnp.exp(s - jnp.max(s, axis=-1, keepdims=True))
    l = jnp.sum(p, axis=-1, keepdims=True)
    return _dot(p.astype(BF16), v.astype(BF16)) / l


def _cross_prompt_kernel(q_ref, k_ref, v_ref, o_ref, *, heads):
    for h in range(heads):
        sl = slice(h * HEAD, (h + 1) * HEAD)
        o_ref[:, sl] = _cross_head(q_ref[:, sl], k_ref[:, sl], v_ref[:, sl])


def _cross_sample_kernel(q_ref, k_ref, v_ref, o_ref, *, heads):
    for h in range(heads):
        sl = slice(h * HEAD, (h + 1) * HEAD)
        o_ref[:, sl] = _cross_head(q_ref[:, sl], k_ref[:, h, :], v_ref[:, h, :])


def _cross_prompt(qc, mk, mv, seq, heads, bq):
    w = qc.shape[1]
    n = mk.shape[0]
    return pl.pallas_call(
        functools.partial(_cross_prompt_kernel, heads=heads),
        grid=(seq // bq,),
        in_specs=[pl.BlockSpec((bq, w), lambda i: (i, 0)),
                  pl.BlockSpec((n, w), lambda i: (0, 0)),
                  pl.BlockSpec((n, w), lambda i: (0, 0))],
        out_specs=pl.BlockSpec((bq, w), lambda i: (i, 0)),
        out_shape=jax.ShapeDtypeStruct((seq, w), F32),
        compiler_params=_cparams(("parallel",)),
        name="cross_prompt",
    )(qc, mk, mv)


def _cross_sample(qc, mem_k, mem_v, layer, seq, t_new):
    depth, batch, n, heads, _ = mem_k.shape
    w = heads * HEAD
    r0 = seq // t_new
    mem_spec = pl.BlockSpec((None, None, n, heads, HEAD), lambda b: (layer, b, 0, 0, 0))
    return pl.pallas_call(
        functools.partial(_cross_sample_kernel, heads=heads),
        grid=(batch,),
        in_specs=[pl.BlockSpec((t_new, w), lambda b: (r0 + b, 0)), mem_spec, mem_spec],
        out_specs=pl.BlockSpec((t_new, w), lambda b: (b, 0)),
        out_shape=jax.ShapeDtypeStruct((batch * t_new, w), F32),
        compiler_params=_cparams(("parallel",)),
        name="cross_sample",
    )(qc, mem_k, mem_v)


def _rope_tables(seq, past_len, batch, t_new):
    half = HEAD // 2
    inv_freq = ROPE_THETA ** (-jnp.arange(half, dtype=F32) / half)
    pos = jnp.concatenate([jnp.arange(seq, dtype=jnp.int32),
                           jnp.tile(past_len + jnp.arange(t_new, dtype=jnp.int32), batch)])
    ang = pos.astype(F32)[:, None] * inv_freq[None, :]
    cos = jnp.cos(ang)
    sin = jnp.sin(ang)
    return jnp.concatenate([cos, cos], axis=-1), jnp.concatenate([-sin, sin], axis=-1)


def kernel(x_prompt, x_sample, cache_k, cache_v, state_hgrn, cache_mem_k, cache_mem_v, page_table, mem_prompt,
           norm_pre, norm_post, ffn_gate, ffn_up, ffn_down, w_in, lower_bounds, hgrn_gnorm,
           w_branch_r, w_branch_a, w_out, mem_norm, w_cq, w_ck, w_cv, w_co):
    b_p, seq, d = x_prompt.shape
    batch, t_new, _ = x_sample.shape
    depth, n_pool, page, h_a, _ = cache_k.shape
    h_r = state_hgrn.shape[2]
    h_m = cache_mem_k.shape[3]
    n_mem = cache_mem_k.shape[2]
    n_pages = page_table.shape[1]
    past_len = n_pages * page
    assert b_p == 1 and h_r == 8 and h_a == 8
    assert seq % MOBA_BLOCK == 0 and past_len % MOBA_BLOCK == 0 and t_new <= MOBA_BLOCK
    assert t_new % 8 == 0 and seq % t_new == 0 and MOBA_BLOCK % page == 0
    m_s = batch * t_new
    m = seq + m_s
    w8 = 8 * HEAD

    bm = _pick(m, (768, 512, 384, 256, 128))
    bm_res = _pick(m_s, (256, 128))
    assert seq % bm_res == 0
    bm_ff = _pick(m, (768, 512, 384, 256, 128))
    bn_ff = _pick(ffn_gate.shape[-1], (512, 256))
    hg_c = _pick(seq, (256,))
    pps = _pick(n_pages, (8, 4, 2))

    cos_t, sin_t = _rope_tables(seq, past_len, batch, t_new)
    lower_bounds = lower_bounds.astype(F32)
    bf = lambda a: a.astype(BF16)
    ffn_down = bf(ffn_down)
    w_branch_r, w_branch_a, w_out = bf(w_branch_r), bf(w_branch_a), bf(w_out)
    w_cq, w_ck, w_cv, w_co = bf(w_cq), bf(w_ck), bf(w_cv), bf(w_co)

    x = (x_prompt.reshape(seq, d), x_sample.reshape(m_s, d))
    u = _norm_cast(x[0], x[1], norm_pre[0, 0][None], bm_res)

    outs = {k: [] for k in ("kp", "vp", "sp", "mkp", "mvp", "ks", "vs", "ss")}
    for l in range(depth):
        g_pre = lambda i: norm_pre[l, i][None]
        g_post = lambda i: norm_post[l, i][None]
        g_next_layer = norm_pre[l + 1, 0][None] if l + 1 < depth else None

        mk_p, mv_p = _memkv(mem_prompt.reshape(n_mem, d), mem_norm[l][None], w_ck, w_cv, l)

        h = _ffn_up(u, ffn_gate, ffn_up, (l, 0), bm_ff, bn_ff)
        x, u = _proj_res(h, ffn_down, (l, 0), x, g_post(0), g_pre(1), 0.5, bm_res, seq)

        pack = _win_hgrn(u, w_in, lower_bounds, l, bm)
        qk, qk_bf = _win_qk(u, w_in, l, cos_t, sin_t, 4, h_a, bm)
        v, vt = _win_v(u, w_in, l, 6, h_a, bm)
        gates = _win_gates(u, w_in, l, 7, 2 * d // w8, bm)

        o_r_p, s_p = _hgrn_prompt(pack, hgrn_gnorm, l, seq, h_r, hg_c, HGRN_HEADS_PER_STEP)
        o_r_s, s_s = _hgrn_sample(pack, hgrn_gnorm, state_hgrn, l, seq, batch, h_r, t_new)

        o_a_p = _moba_prompt(qk, qk_bf, vt, seq, h_a)
        o_a_s = _moba_sample(cache_k, cache_v, l, page_table, qk, v, seq, t_new, pps)

        merged = _merge((o_r_p, o_r_s), (o_a_p, o_a_s), gates, w_branch_r, w_branch_a, l, bm_res, seq)
        x, u = _proj_res(merged, w_out, (l,), x, g_post(1), g_pre(2), 1.0, bm_res, seq)

        qc = _scaled_proj(u, w_cq, (l,), HEAD ** -0.5, bm)
        oc_p = _cross_prompt(qc, mk_p, mv_p, seq, h_m, _pick(seq, (512, 256)))
        oc_s = _cross_sample(qc, cache_mem_k, cache_mem_v, l, seq, t_new)
        x, u = _proj_res((oc_p, oc_s), w_co, (l,), x, g_post(2), g_pre(3), 1.0, bm_res, seq)

        h = _ffn_up(u, ffn_gate, ffn_up, (l, 1), bm_ff, bn_ff)
        res = _proj_res(h, ffn_down, (l, 1), x, g_post(3), g_next_layer, 0.5, bm_res, seq)
        if g_next_layer is None:
            x = tuple(res)
        else:
            x, u = res

        k_all = qk[:, w8:]
        outs["kp"].append(k_all[:seq].reshape(1, seq, h_a, HEAD))
        outs["vp"].append(v[:seq].reshape(1, seq, h_a, HEAD))
        outs["sp"].append(s_p.reshape(1, h_r, HEAD, HEAD))
        outs["mkp"].append(mk_p.reshape(1, n_mem, h_m, HEAD))
        outs["mvp"].append(mv_p.reshape(1, n_mem, h_m, HEAD))
        outs["ks"].append(k_all[seq:].reshape(batch, t_new, h_a, HEAD))
        outs["vs"].append(v[seq:].reshape(batch, t_new, h_a, HEAD))
        outs["ss"].append(s_s)

    st = lambda k: jnp.stack(outs[k])
    return (x[0].reshape(1, seq, d), x[1].reshape(batch, t_new, d),
            st("kp"), st("vp"), st("sp"), st("mkp"), st("mvp"), st("ks"), st("vs"), st("ss"))
```

```python
import functools

import numpy as np
import jax
import jax.numpy as jnp
from jax import lax
from jax.experimental import pallas as pl
from jax.experimental.pallas import tpu as pltpu

F32 = jnp.float32
BF16 = jnp.bfloat16

MOBA_BLOCK = 256
MOBA_TOPK = 3
ROPE_THETA = 10000.0
EPS = 1e-6
HEAD = 128
LANES = 128
NEG = -1e30
LOG2E = 1.4426950408889634
VMEM_LIMIT = 56 * 1024 * 1024
HGRN_HEADS_PER_STEP = 8


def _cparams(sem, vmem=VMEM_LIMIT):
    return pltpu.CompilerParams(dimension_semantics=sem, vmem_limit_bytes=vmem)


def _pick(n, cands):
    for c in cands:
        if n % c == 0:
            return c
    raise ValueError(f"no tile in {cands} divides {n}")


def _wspec(lead, shape, index_map, resident=False):
    full_shape = (None,) * len(lead) + tuple(shape)
    full_map = lambda *a: tuple(lead) + tuple(index_map(*a))
    if resident:
        return pl.BlockSpec(full_shape, full_map, pipeline_mode=pl.Buffered(1))
    return pl.BlockSpec(full_shape, full_map)


def _two_src_specs(bm, k, n_p):
    return [pl.BlockSpec((bm, k), lambda i: (jnp.minimum(i, n_p - 1), 0)),
            pl.BlockSpec((bm, k), lambda i: (jnp.maximum(i - n_p, 0), 0))]


def _two_src_load(refs, n_p):
    return jnp.where(pl.program_id(0) < n_p, refs[0][...], refs[1][...])


def _dot(a, b):
    return jnp.dot(a, b, preferred_element_type=F32)


def _dot_nt(a, b):
    return lax.dot_general(a, b, (((1,), (1,)), ((), ())), preferred_element_type=F32)


def _split2(a):
    hi = a.astype(BF16)
    lo = (a - hi.astype(F32)).astype(BF16)
    return hi, lo


def _dot3_nt(a, b):
    ah, al = _split2(a)
    bh, bl = _split2(b)
    return _dot_nt(ah, bh) + _dot_nt(ah, bl) + _dot_nt(al, bh)


def _dot3(a, b):
    ah, al = _split2(a)
    bh, bl = _split2(b)
    return _dot(ah, bh) + _dot(ah, bl) + _dot(al, bh)


def _sigmoid(x):
    return 1.0 / (1.0 + jnp.exp(-x))


def _silu(x):
    return x * _sigmoid(x)


def _rms(y, g):
    ms = jnp.mean(y * y, axis=-1, keepdims=True)
    return y * lax.rsqrt(ms + EPS) * g


def _cast_once(w_ref, wb_ref, axis):
    @pl.when(pl.program_id(axis) == 0)
    def _():
        wb_ref[...] = w_ref[...].astype(BF16)


def _norm_kernel(xp_ref, xs_ref, g_ref, u_ref, *, n_p):
    u_ref[...] = _rms(_two_src_load((xp_ref, xs_ref), n_p), g_ref[...]).astype(BF16)


def _norm_cast(x_p, x_s, g, bm):
    seq, d = x_p.shape
    m = seq + x_s.shape[0]
    n_p = seq // bm
    return pl.pallas_call(
        functools.partial(_norm_kernel, n_p=n_p),
        grid=(m // bm,),
        in_specs=_two_src_specs(bm, d, n_p) + [pl.BlockSpec((1, d), lambda i: (0, 0))],
        out_specs=pl.BlockSpec((bm, d), lambda i: (i, 0)),
        out_shape=jax.ShapeDtypeStruct((m, d), BF16),
        compiler_params=_cparams(("parallel",)),
        name="norm_cast",
    )(x_p, x_s, g)


def _ffn_up_kernel(u_ref, wg_ref, wu_ref, wd_ref, h_ref, wdb_ref, wgb_ref, wub_ref):
    _cast_once(wg_ref, wgb_ref, 1)
    _cast_once(wu_ref, wub_ref, 1)
    _cast_once(wd_ref, wdb_ref, 1)
    u = u_ref[...]
    h_ref[...] = (_silu(_dot(u, wgb_ref[...])) * _dot(u, wub_ref[...])).astype(BF16)


def _ffn_up(u, wg, wu, wd, lead, bm, bn):
    m, d = u.shape
    f = wg.shape[-1]
    return pl.pallas_call(
        _ffn_up_kernel,
        grid=(f // bn, m // bm),
        in_specs=[pl.BlockSpec((bm, d), lambda n, i: (i, 0)),
                  _wspec(lead, (d, bn), lambda n, i: (0, n)),
                  _wspec(lead, (d, bn), lambda n, i: (0, n)),
                  _wspec(lead, (bn, d), lambda n, i: (n, 0))],
        out_specs=[pl.BlockSpec((bm, bn), lambda n, i: (i, n)),
                   pl.BlockSpec((bn, d), lambda n, i: (n, 0))],
        out_shape=[jax.ShapeDtypeStruct((m, f), BF16), jax.ShapeDtypeStruct((f, d), BF16)],
        scratch_shapes=[pltpu.VMEM((d, bn), BF16), pltpu.VMEM((d, bn), BF16)],
        compiler_params=_cparams(("arbitrary", "arbitrary")),
        name="ffn_up",
    )(u, wg, wu, wd)


def _proj_res_kernel(*refs, scale, a2, x2, split, n_p, cast_w):
    refs = list(refs)
    a_refs = [refs.pop(0) for _ in range(2 if a2 else 1)]
    w_ref = refs.pop(0)
    if cast_w:
        wb_ref = refs.pop()
        _cast_once(w_ref, wb_ref, 0)
        w_ref = wb_ref
    x_refs = [refs.pop(0) for _ in range(2 if x2 else 1)]
    gp_ref = refs.pop(0)
    a = _two_src_load(a_refs, n_p) if a2 else a_refs[0][...]
    x = _two_src_load(x_refs, n_p) if x2 else x_refs[0][...]
    xn = x + scale * _rms(_dot(a.astype(BF16), w_ref[...]), gp_ref[...])
    if split:
        xop_ref, xos_ref = refs
        is_p = pl.program_id(0) < n_p

        @pl.when(is_p)
        def _():
            xop_ref[...] = xn

        @pl.when(jnp.logical_not(is_p))
        def _():
            xos_ref[...] = xn
    else:
        gn_ref, xo_ref, uo_ref = refs
        xo_ref[...] = xn
        uo_ref[...] = _rms(xn, gn_ref[...]).astype(BF16)


def _proj_res(a, w, lead, x, g_post, g_next, scale, bm, seq):
    a2, x2, split = isinstance(a, tuple), isinstance(x, tuple), g_next is None
    cast_w = w.dtype != BF16
    k, d = w.shape[-2:]
    n_p = seq // bm
    m = (x[0].shape[0] + x[1].shape[0]) if x2 else x.shape[0]
    row = lambda c: pl.BlockSpec((bm, c), lambda i: (i, 0))
    vec = pl.BlockSpec((1, d), lambda i: (0, 0))
    in_specs = (_two_src_specs(bm, k, n_p) if a2 else [row(k)])
    in_specs += [_wspec(lead, (k, d), lambda i: (0, 0), resident=True)]
    in_specs += (_two_src_specs(bm, d, n_p) if x2 else [row(d)]) + [vec]
    args = (list(a) if a2 else [a]) + [w] + (list(x) if x2 else [x]) + [g_post]
    if split:
        out_specs = _two_src_specs(bm, d, n_p)
        out_shape = [jax.ShapeDtypeStruct((seq, d), F32), jax.ShapeDtypeStruct((m - seq, d), F32)]
    else:
        in_specs += [vec]
        args += [g_next]
        out_specs = [row(d), row(d)]
        out_shape = [jax.ShapeDtypeStruct((m, d), F32), jax.ShapeDtypeStruct((m, d), BF16)]
    return pl.pallas_call(
        functools.partial(_proj_res_kernel, scale=scale, a2=a2, x2=x2, split=split, n_p=n_p, cast_w=cast_w),
        grid=(m // bm,),
        in_specs=in_specs,
        out_specs=out_specs,
        out_shape=out_shape,
        scratch_shapes=[pltpu.VMEM((k, d), BF16)] if cast_w else [],
        compiler_params=_cparams(("arbitrary",)),
        name="proj_res",
    )(*args)


def _win_hgrn_kernel(u_ref, w_ref, lb_ref, o_ref, wb_ref, *, layer):
    _cast_once(w_ref, wb_ref, 1)
    n = pl.program_id(0)
    acc = _dot(u_ref[...], wb_ref[...])

    @pl.when(n == 0)
    def _():
        o_ref[...] = _silu(acc) * (HEAD ** -0.5)

    @pl.when(n == 1)
    def _():
        lbs = lb_ref[...]
        e = jnp.exp(lbs - jnp.max(lbs, axis=0, keepdims=True))
        sm = e / jnp.sum(e, axis=0, keepdims=True)
        cum = sm[0:1, :]
        for i in range(1, layer + 1):
            cum = cum + sm[i:i + 1, :]
        lb = cum - sm[0:1, :]
        ls = jnp.minimum(acc, 0.0) - jnp.log1p(jnp.exp(-jnp.abs(acc)))
        a = jnp.log(lb)
        b = jnp.log1p(-lb) + ls
        o_ref[...] = jnp.maximum(a, b) + jnp.log1p(jnp.exp(-jnp.abs(a - b)))

    @pl.when(n == 2)
    def _():
        o_ref[...] = acc

    @pl.when(n == 3)
    def _():
        o_ref[...] = _silu(acc)


def _win_hgrn(u, w_in, lower_bounds, layer, bm):
    m, d = u.shape
    w = 8 * HEAD
    depth = lower_bounds.shape[0]
    return pl.pallas_call(
        functools.partial(_win_hgrn_kernel, layer=layer),
        grid=(4, m // bm),
        in_specs=[pl.BlockSpec((bm, d), lambda n, i: (i, 0)),
                  _wspec((layer,), (d, w), lambda n, i: (0, n)),
                  pl.BlockSpec((depth, w), lambda n, i: (0, 0))],
        out_specs=pl.BlockSpec((bm, w), lambda n, i: (i, n)),
        out_shape=jax.ShapeDtypeStruct((m, 4 * w), F32),
        scratch_shapes=[pltpu.VMEM((d, w), BF16)],
        compiler_params=_cparams(("arbitrary", "arbitrary")),
        name="win_hgrn",
    )(u, w_in, lower_bounds)


def _win_rope_kernel(u_ref, w_ref, cos_ref, sin_ref, o_ref, *rest, heads):
    wb_ref = rest[-1]
    _cast_once(w_ref, wb_ref, 0)
    acc = _dot(u_ref[...], wb_ref[...])
    c = cos_ref[...]
    s = sin_ref[...]
    for h in range(heads):
        sl = slice(h * HEAD, (h + 1) * HEAD)
        xh = acc[:, sl]
        r = xh * c + pltpu.roll(xh, HEAD // 2, 1) * s
        o_ref[:, sl] = r
        if len(rest) == 2:
            rest[0][:, sl] = r.astype(BF16)


def _win_rope(u, w_in, layer, cos_t, sin_t, col_blk, heads, bm, with_bf16):
    m, d = u.shape
    w = heads * HEAD
    row = pl.BlockSpec((bm, w), lambda i: (i, 0))
    return pl.pallas_call(
        functools.partial(_win_rope_kernel, heads=heads),
        grid=(m // bm,),
        in_specs=[pl.BlockSpec((bm, d), lambda i: (i, 0)),
                  _wspec((layer,), (d, w), lambda i: (0, col_blk), resident=True),
                  pl.BlockSpec((bm, HEAD), lambda i: (i, 0)),
                  pl.BlockSpec((bm, HEAD), lambda i: (i, 0))],
        out_specs=[row, row] if with_bf16 else [row],
        out_shape=([jax.ShapeDtypeStruct((m, w), F32)] +
                   ([jax.ShapeDtypeStruct((m, w), BF16)] if with_bf16 else [])),
        scratch_shapes=[pltpu.VMEM((d, w), BF16)],
        compiler_params=_cparams(("arbitrary",)),
        name="win_rope",
    )(u, w_in, cos_t, sin_t)


def _win_v_kernel(u_ref, w_ref, v_ref, vt_ref, wb_ref):
    _cast_once(w_ref, wb_ref, 0)
    acc = _dot(u_ref[...], wb_ref[...])
    v_ref[...] = acc
    vt_ref[...] = acc.T.astype(BF16)


def _win_v(u, w_in, layer, col_blk, heads, bm):
    m, d = u.shape
    w = heads * HEAD
    return pl.pallas_call(
        _win_v_kernel,
        grid=(m // bm,),
        in_specs=[pl.BlockSpec((bm, d), lambda i: (i, 0)),
                  _wspec((layer,), (d, w), lambda i: (0, col_blk), resident=True)],
        out_specs=[pl.BlockSpec((bm, w), lambda i: (i, 0)), pl.BlockSpec((w, bm), lambda i: (0, i))],
        out_shape=[jax.ShapeDtypeStruct((m, w), F32), jax.ShapeDtypeStruct((w, m), BF16)],
        scratch_shapes=[pltpu.VMEM((d, w), BF16)],
        compiler_params=_cparams(("arbitrary",)),
        name="win_v",
    )(u, w_in)


def _win_gate_kernel(u_ref, w_ref, o_ref, wb_ref):
    _cast_once(w_ref, wb_ref, 1)
    o_ref[...] = _sigmoid(_dot(u_ref[...], wb_ref[...])).astype(BF16)


def _win_gates(u, w_in, layer, col_blk, n_blk, bm):
    m, d = u.shape
    w = 8 * HEAD
    return pl.pallas_call(
        _win_gate_kernel,
        grid=(n_blk, m // bm),
        in_specs=[pl.BlockSpec((bm, d), lambda n, i: (i, 0)),
                  _wspec((layer,), (d, w), lambda n, i: (0, col_blk + n))],
        out_specs=pl.BlockSpec((bm, w), lambda n, i: (i, n)),
        out_shape=jax.ShapeDtypeStruct((m, n_blk * w), BF16),
        scratch_shapes=[pltpu.VMEM((d, w), BF16)],
        compiler_params=_cparams(("arbitrary", "arbitrary")),
        name="win_gates",
    )(u, w_in)


def _scaled_proj_kernel(u_ref, w_ref, o_ref, wb_ref, *, scale):
    _cast_once(w_ref, wb_ref, 0)
    o_ref[...] = _dot(u_ref[...], wb_ref[...]) * scale


def _scaled_proj(u, w, lead, scale, bm):
    m, d = u.shape
    n = w.shape[-1]
    return pl.pallas_call(
        functools.partial(_scaled_proj_kernel, scale=scale),
        grid=(m // bm,),
        in_specs=[pl.BlockSpec((bm, d), lambda i: (i, 0)),
                  _wspec(lead, (d, n), lambda i: (0, 0), resident=True)],
        out_specs=pl.BlockSpec((bm, n), lambda i: (i, 0)),
        out_shape=jax.ShapeDtypeStruct((m, n), F32),
        scratch_shapes=[pltpu.VMEM((d, n), BF16)],
        compiler_params=_cparams(("arbitrary",)),
        name="scaled_proj",
    )(u, w)


def _memkv_kernel(mem_ref, g_ref, wk_ref, wv_ref, k_ref, v_ref):
    mn = _rms(mem_ref[...], g_ref[...]).astype(BF16)
    k_ref[...] = _dot(mn, wk_ref[...].astype(BF16))
    v_ref[...] = _dot(mn, wv_ref[...].astype(BF16))


def _memkv(mem, g, wk, wv, layer):
    n, d = mem.shape
    w = wk.shape[-1]
    return pl.pallas_call(
        _memkv_kernel,
        grid=(1,),
        in_specs=[pl.BlockSpec((n, d), lambda i: (0, 0)),
                  pl.BlockSpec((1, d), lambda i: (0, 0)),
                  _wspec((layer,), (d, w), lambda i: (0, 0)),
                  _wspec((layer,), (d, w), lambda i: (0, 0))],
        out_specs=[pl.BlockSpec((n, w), lambda i: (0, 0)), pl.BlockSpec((n, w), lambda i: (0, 0))],
        out_shape=[jax.ShapeDtypeStruct((n, w), F32), jax.ShapeDtypeStruct((n, w), F32)],
        compiler_params=_cparams(("arbitrary",)),
        name="memkv",
    )(mem, g, wk, wv)


def _cumsum_rows(g, c):
    row = lax.broadcasted_iota(jnp.int32, g.shape, 0)
    out = g
    sh = 1
    while sh < c:
        out = out + jnp.where(row >= sh, pltpu.roll(out, sh, 0), 0.0)
        sh *= 2
    return out


def _hgrn_chunk(q, g, v, st, c):
    g = g * LOG2E
    k = 1.0 - jnp.exp2(g)
    gc = _cumsum_rows(g, c)
    row = lax.broadcasted_iota(jnp.int32, (c, HEAD), 0)
    ra = lax.broadcasted_iota(jnp.int32, (c, c), 0)
    ca = lax.broadcasted_iota(jnp.int32, (c, c), 1)
    x = jnp.bitwise_xor(ra, ca)

    a = jnp.zeros((c, c), F32)
    b = c // 2
    while b >= 1:
        upper = jnp.bitwise_and(row, b) != 0
        if b >= 4:
            nb = c // (2 * b)
            gcb = gc.reshape(nb, 2 * b, HEAD)
            r = gcb[:, b - 1:b, :]
            up3 = upper.reshape(nb, 2 * b, HEAD)
            dlt = jnp.where(up3, gcb - r, r - gcb).reshape(c, HEAD)
        elif b == 2:
            t4 = jnp.bitwise_and(row, 3)
            g_next = pltpu.roll(g, c - 1, 0)
            g_prev = pltpu.roll(g, 1, 0)
            dlt = jnp.where(t4 == 0, g_next, jnp.where(t4 == 1, 0.0, jnp.where(t4 == 2, g, g + g_prev)))
        else:
            dlt = jnp.where(upper, g, 0.0)
        z = (jnp.where(upper, q, k) * jnp.exp2(dlt)).astype(BF16)
        p = _dot_nt(z, z)
        a = p if b == c // 2 else jnp.where(x < 2 * b, p, a)
        b //= 2
    a = jnp.where(ra > ca, a, 0.0)

    vb = v.astype(BF16)
    o = _dot(a.astype(BF16), vb)
    o = o + jnp.sum(q * k, axis=-1, keepdims=True) * v
    o = o + _dot_nt((q * jnp.exp2(gc)).astype(BF16), st.astype(BF16))
    g_end = gc[c - 1:c, :]
    khat = (k * jnp.exp2(g_end - gc)).astype(BF16)
    st_new = st * jnp.exp2(g_end) + lax.dot_general(vb, khat, (((0,), (0,)), ((), ())),
                                                    preferred_element_type=F32)
    return o, st_new


def _hgrn_out(o, gn, og):
    return _rms(o, gn) * og


def _hgrn_prompt_kernel(q_ref, g_ref, v_ref, og_ref, gn_ref, o_ref, s_ref, st_ref, *, c, hpb):
    hg = pl.program_id(0)
    ci = pl.program_id(1)

    @pl.when(ci == 0)
    def _():
        st_ref[...] = jnp.zeros_like(st_ref)

    gn_all = gn_ref[...]
    hrow = lax.broadcasted_iota(jnp.int32, gn_all.shape, 0)
    sls = [slice(j * HEAD, (j + 1) * HEAD) for j in range(hpb)]
    res = [_hgrn_chunk(q_ref[:, sl], g_ref[:, sl], v_ref[:, sl], st_ref[j], c) for j, sl in enumerate(sls)]
    for j, (o, st_new) in enumerate(res):
        gn = jnp.sum(jnp.where(hrow == hg * hpb + j, gn_all, 0.0), axis=0, keepdims=True)
        st_ref[j] = st_new
        o_ref[:, sls[j]] = _hgrn_out(o, gn, og_ref[:, sls[j]])

    @pl.when(ci == pl.num_programs(1) - 1)
    def _():
        for j, (_, st_new) in enumerate(res):
            s_ref[j] = st_new.T


def _hgrn_prompt(pack, gnorm, layer, seq, heads, c, hpb):
    n_hg = heads // hpb
    blk = lambda off: pl.BlockSpec((c, hpb * HEAD), lambda h, i: (i, off * n_hg + h))
    return pl.pallas_call(
        functools.partial(_hgrn_prompt_kernel, c=c, hpb=hpb),
        grid=(n_hg, seq // c),
        in_specs=[blk(0), blk(1), blk(2), blk(3),
                  _wspec((layer,), (heads, HEAD), lambda h, i: (0, 0))],
        out_specs=[pl.BlockSpec((c, hpb * HEAD), lambda h, i: (i, h)),
                   pl.BlockSpec((hpb, HEAD, HEAD), lambda h, i: (h, 0, 0))],
        out_shape=[jax.ShapeDtypeStruct((seq, heads * HEAD), F32),
                   jax.ShapeDtypeStruct((heads, HEAD, HEAD), F32)],
        scratch_shapes=[pltpu.VMEM((hpb, HEAD, HEAD), F32)],
        compiler_params=_cparams(("parallel", "arbitrary")),
        name="hgrn_prompt",
    )(pack, pack, pack, pack, gnorm)


def _hgrn_sample_kernel(q_ref, g_ref, v_ref, og_ref, gn_ref, s0_ref, o_ref, s_ref, *, c, heads):
    for h in range(heads):
        sl = slice(h * HEAD, (h + 1) * HEAD)
        o, st_new = _hgrn_chunk(q_ref[:, sl], g_ref[:, sl], v_ref[:, sl], s0_ref[h].T, c)
        o_ref[:, sl] = _hgrn_out(o, gn_ref[h:h + 1, :], og_ref[:, sl])
        s_ref[h] = st_new.T


def _hgrn_sample(pack, gnorm, s0, layer, seq, batch, heads, c):
    w = heads * HEAD
    r0 = seq // c
    blk = lambda off: pl.BlockSpec((c, w), lambda b: (r0 + b, off))
    return pl.pallas_call(
        functools.partial(_hgrn_sample_kernel, c=c, heads=heads),
        grid=(batch,),
        in_specs=[blk(0), blk(1), blk(2), blk(3),
                  _wspec((layer,), (heads, HEAD), lambda b: (0, 0)),
                  _wspec((layer,), (None, heads, HEAD, HEAD), lambda b: (b, 0, 0, 0))],
        out_specs=[pl.BlockSpec((c, w), lambda b: (b, 0)),
                   pl.BlockSpec((None, heads, HEAD, HEAD), lambda b: (b, 0, 0, 0))],
        out_shape=[jax.ShapeDtypeStruct((batch * c, w), F32),
                   jax.ShapeDtypeStruct(s0.shape[1:], F32)],
        compiler_params=_cparams(("parallel",)),
        name="hgrn_sample",
    )(pack, pack, pack, pack, gnorm, s0)


def _topk_mask(score, valid, axis):
    idx = lax.broadcasted_iota(jnp.int32, score.shape, axis)
    big = jnp.int32(2 ** 30)
    sg = jnp.where(valid, score, -jnp.inf)
    sel = jnp.zeros(score.shape, F32)
    for _ in range(MOBA_TOPK):
        mx = jnp.max(sg, axis=axis, keepdims=True)
        first = jnp.min(jnp.where(sg == mx, idx, big), axis=axis, keepdims=True)
        pick = jnp.logical_and(idx == first, mx > -jnp.inf)
        sel = jnp.where(pick, 1.0, sel)
        sg = jnp.where(pick, -jnp.inf, sg)
    return sel


def _moba_prompt_kernel(it_ref, jt_ref, fl_ref, q_ref, ko_ref, vto_ref, kp_ref, vtp_ref, o_ref,
                        qs_ref, m_ref, l_ref, acc_ref, sel_ref, km_ref, *, heads, nbp):
    s = pl.program_id(0)
    i = it_ref[s]
    jp = jt_ref[s]
    own = jnp.bitwise_and(fl_ref[s], 1) == 1
    last = jnp.bitwise_and(fl_ref[s], 2) == 2
    blk = MOBA_BLOCK
    hs = [slice(h * HEAD, (h + 1) * HEAD) for h in range(heads)]

    @pl.when(s == 0)
    def _():
        km_ref[...] = jnp.zeros_like(km_ref)

    @pl.when(own)
    def _():
        k = ko_ref[...]
        kb = k.astype(BF16)
        km_ref[pl.ds(i, 1), :] = jnp.mean(k, axis=0, keepdims=True)
        qt = q_ref[...].T
        qs_ref[...] = (qt * (HEAD ** -0.5 * LOG2E)).astype(BF16)
        key = lax.broadcasted_iota(jnp.int32, (blk, blk), 0)
        qry = lax.broadcasted_iota(jnp.int32, (blk, blk), 1)
        causal = jnp.where(key <= qry, 0.0, NEG)
        bidx = lax.broadcasted_iota(jnp.int32, (nbp, blk), 0)
        km = km_ref[...]
        qsb = qs_ref[...]
        gates = [_dot3(km[:, sl], qt[sl, :]) for sl in hs]
        sts = [_dot(kb[:, sl], qsb[sl, :]) for sl in hs]
        m_rows, l_rows, ps = [], [], []
        for h in range(heads):
            st = sts[h] + causal
            m = jnp.max(st, axis=0, keepdims=True)
            p = jnp.exp2(st - m)
            m_rows.append(m)
            l_rows.append(jnp.sum(p, axis=0, keepdims=True))
            ps.append(p.astype(BF16))
        for h, sl in enumerate(hs):
            acc_ref[sl, :] = _dot(vto_ref[sl, :], ps[h])
            sel_ref[h] = _topk_mask(gates[h], bidx < i, 0)
        m_ref[...] = jnp.concatenate(m_rows, axis=0)
        l_ref[...] = jnp.concatenate(l_rows, axis=0)

    @pl.when(jnp.logical_not(own))
    def _():
        kb = kp_ref[...]
        m_all = m_ref[...]
        l_all = l_ref[...]
        sel_a = sel_ref[:, pl.ds(2 * jp, 1), :]
        sel_b = sel_ref[:, pl.ds(2 * jp + 1, 1), :]
        sts = [_dot(kb[:, sl], qs_ref[sl, :]) for sl in hs]
        m_rows, l_rows, ps, alphas = [], [], [], []
        for h in range(heads):
            st_a = sts[h][:blk] + (sel_a[h] - 1.0) * (-NEG)
            st_b = sts[h][blk:] + (sel_b[h] - 1.0) * (-NEG)
            m_old = m_all[h:h + 1, :]
            m_new = jnp.maximum(m_old, jnp.maximum(jnp.max(st_a, axis=0, keepdims=True),
                                                   jnp.max(st_b, axis=0, keepdims=True)))
            alpha = jnp.exp2(m_old - m_new)
            p_a = jnp.exp2(st_a - m_new)
            p_b = jnp.exp2(st_b - m_new)
            m_rows.append(m_new)
            l_rows.append(alpha * l_all[h:h + 1, :] + jnp.sum(p_a, axis=0, keepdims=True)
                          + jnp.sum(p_b, axis=0, keepdims=True))
            ps.append(jnp.concatenate([p_a.astype(BF16), p_b.astype(BF16)], axis=0))
            alphas.append(alpha)
        for h, sl in enumerate(hs):
            acc_ref[sl, :] = alphas[h] * acc_ref[sl, :] + _dot(vtp_ref[sl, :], ps[h])
        m_ref[...] = jnp.concatenate(m_rows, axis=0)
        l_ref[...] = jnp.concatenate(l_rows, axis=0)

    @pl.when(last)
    def _():
        for h, sl in enumerate(hs):
            o_ref[:, sl] = (acc_ref[sl, :] / l_ref[h:h + 1, :]).T


def _moba_prompt(q, k, k_bf, vt, seq, heads):
    w = heads * HEAD
    blk = MOBA_BLOCK
    nb = seq // blk
    assert nb % 2 == 0
    nbp = -(-nb // 8) * 8
    it, jt, fl = [], [], []
    for i in range(nb):
        it.append(i)
        jt.append(0)
        fl.append(1)
        for jp in range((i + 1) // 2):
            it.append(i)
            jt.append(jp)
            fl.append(0)
        fl[-1] += 2
    tabs = [jnp.asarray(np.asarray(t, np.int32)) for t in (it, jt, fl)]
    grid_spec = pltpu.PrefetchScalarGridSpec(
        num_scalar_prefetch=3,
        grid=(len(it),),
        in_specs=[pl.BlockSpec((blk, w), lambda s, it, jt, fl: (it[s], 0)),
                  pl.BlockSpec((blk, w), lambda s, it, jt, fl: (it[s], 0)),
                  pl.BlockSpec((w, blk), lambda s, it, jt, fl: (0, it[s])),
                  pl.BlockSpec((2 * blk, w), lambda s, it, jt, fl: (jt[s], 0)),
                  pl.BlockSpec((w, 2 * blk), lambda s, it, jt, fl: (0, jt[s]))],
        out_specs=pl.BlockSpec((blk, w), lambda s, it, jt, fl: (it[s], 0)),
        scratch_shapes=[pltpu.VMEM((w, blk), BF16),
                        pltpu.VMEM((heads, blk), F32),
                        pltpu.VMEM((heads, blk), F32),
                        pltpu.VMEM((w, blk), F32),
                        pltpu.VMEM((heads, nbp, blk), F32),
                        pltpu.VMEM((nbp, w), F32)],
    )
    return pl.pallas_call(
        functools.partial(_moba_prompt_kernel, heads=heads, nbp=nbp),
        grid_spec=grid_spec,
        out_shape=jax.ShapeDtypeStruct((seq, w), F32),
        compiler_params=_cparams(("arbitrary",)),
        name="moba_prompt",
    )(*tabs, q, k, vt, k_bf, vt)


def _moba_sample_kernel(pt_ref, *refs, heads, t_new, pps, nb, page):
    del pt_ref
    k_refs = refs[:pps]
    v_refs = refs[pps:2 * pps]
    q_ref, kn_ref, vn_ref, o_ref, m_ref, l_ref, ob_ref, km_ref = refs[2 * pps:]
    step = pl.program_id(1)
    ppb = MOBA_BLOCK // page
    bps = pps // ppb
    rows = heads * t_new
    cols = ppb * page * heads

    q = q_ref[...]
    q_all = jnp.concatenate([q[:, h * HEAD:(h + 1) * HEAD] for h in range(heads)], axis=0)
    qs = (q_all * (HEAD ** -0.5 * LOG2E)).astype(BF16)

    rh = lax.broadcasted_iota(jnp.int32, (rows, cols), 0) // t_new
    ch = jnp.bitwise_and(lax.broadcasted_iota(jnp.int32, (rows, cols), 1), heads - 1)
    head_bias = jnp.where(rh == ch, 0.0, NEG)

    blocks = [range(bi * ppb, (bi + 1) * ppb) for bi in range(bps)]
    scs = []
    for pages in blocks:
        k2 = jnp.concatenate([k_refs[p][...].reshape(page * heads, HEAD) for p in pages], axis=0)
        scs.append(_dot_nt(qs, k2.astype(BF16)))
    ms, ls, ps = [], [], []
    for sc in scs:
        sc = sc + head_bias
        m = sc.max(axis=-1, keepdims=True)
        p_exp = jnp.exp2(sc - m)
        ms.append(m)
        ls.append(p_exp.sum(axis=-1, keepdims=True))
        ps.append(p_exp.astype(BF16))
    for bi, pages in enumerate(blocks):
        jb = step * bps + bi
        v2 = jnp.concatenate([v_refs[p][...].reshape(page * heads, HEAD) for p in pages], axis=0)
        ob_ref[jb] = _dot(ps[bi], v2.astype(BF16))
        m_ref[jb] = jnp.broadcast_to(ms[bi], (rows, LANES))
        l_ref[jb] = jnp.broadcast_to(ls[bi], (rows, LANES))
        ksum = k_refs[pages[0]][...].sum(axis=0)
        for p in pages[1:]:
            ksum = ksum + k_refs[p][...].sum(axis=0)
        km_ref[jb] = ksum * (1.0 / MOBA_BLOCK)

    @pl.when(step == pl.num_programs(1) - 1)
    def _():
        tq = lax.broadcasted_iota(jnp.int32, (t_new, LANES), 0)
        kk = lax.broadcasted_iota(jnp.int32, (t_new, LANES), 1)
        causal = jnp.where(kk <= tq, 0.0, NEG)
        zpad = jnp.zeros((LANES - t_new, HEAD), F32)
        m_o, l_o, o_o = [], [], []
        for h in range(heads):
            sl = slice(h * HEAD, (h + 1) * HEAD)
            k_own = jnp.concatenate([kn_ref[:, sl], zpad], axis=0).astype(BF16)
            v_own = jnp.concatenate([vn_ref[:, sl], zpad], axis=0).astype(BF16)
            s_o = _dot_nt(qs[h * t_new:(h + 1) * t_new, :], k_own) + causal
            mh = s_o.max(axis=-1, keepdims=True)
            ph = jnp.exp2(s_o - mh)
            m_o.append(mh)
            l_o.append(ph.sum(axis=-1, keepdims=True))
            o_o.append(_dot(ph.astype(BF16), v_own))
        m_o, l_o, o_o = (jnp.concatenate(x, axis=0) for x in (m_o, l_o, o_o))
        gate = _dot3_nt(q_all, km_ref[...].reshape(nb * heads, HEAD))
        gr = lax.broadcasted_iota(jnp.int32, (rows, nb * heads), 0) // t_new
        gc = jnp.bitwise_and(lax.broadcasted_iota(jnp.int32, (rows, nb * heads), 1), heads - 1)
        sel = _topk_mask(gate, gr == gc, 1)
        picked = [sel[:, jb * heads:(jb + 1) * heads].max(axis=-1, keepdims=True) > 0.5 for jb in range(nb)]
        m_all = m_o
        for jb in range(nb):
            m_all = jnp.maximum(m_all, jnp.where(picked[jb], m_ref[jb][:, 0:1], NEG))
        w_o = jnp.exp2(m_o - m_all)
        l_tot = w_o * l_o
        o_tot = w_o * o_o
        for jb in range(nb):
            wj = jnp.where(picked[jb], jnp.exp2(jnp.where(picked[jb], m_ref[jb][:, 0:1], NEG) - m_all), 0.0)
            l_tot = l_tot + wj * l_ref[jb][:, 0:1]
            o_tot = o_tot + wj * ob_ref[jb]
        res = o_tot / l_tot
        for h in range(heads):
            o_ref[:, h * HEAD:(h + 1) * HEAD] = res[h * t_new:(h + 1) * t_new, :]


def _moba_sample(cache_k, cache_v, layer, page_table, q, k, v, seq, t_new, pps):
    depth, n_pool, page, heads, _ = cache_k.shape
    batch, n_pages = page_table.shape
    assert heads & (heads - 1) == 0
    w = heads * HEAD
    nb = n_pages * page // MOBA_BLOCK
    n_steps = n_pages // pps
    r0 = seq // t_new
    rows = heads * t_new

    def page_spec(p):
        return pl.BlockSpec((None, None, page, heads, HEAD),
                            lambda b, t, pt: (layer, pt[b * n_pages + t * pps + p], 0, 0, 0))

    grid_spec = pltpu.PrefetchScalarGridSpec(
        num_scalar_prefetch=1,
        grid=(batch, n_steps),
        in_specs=([page_spec(p) for p in range(pps)] + [page_spec(p) for p in range(pps)] +
                  [pl.BlockSpec((t_new, w), lambda b, t, pt: (r0 + b, 0)),
                   pl.BlockSpec((t_new, w), lambda b, t, pt: (r0 + b, 0)),
                   pl.BlockSpec((t_new, w), lambda b, t, pt: (r0 + b, 0))]),
        out_specs=pl.BlockSpec((t_new, w), lambda b, t, pt: (b, 0)),
        scratch_shapes=[pltpu.VMEM((nb, rows, LANES), F32),
                        pltpu.VMEM((nb, rows, LANES), F32),
                        pltpu.VMEM((nb, rows, HEAD), F32),
                        pltpu.VMEM((nb, heads, HEAD), F32)],
    )
    return pl.pallas_call(
        functools.partial(_moba_sample_kernel, heads=heads, t_new=t_new, pps=pps, nb=nb, page=page),
        grid_spec=grid_spec,
        out_shape=jax.ShapeDtypeStruct((batch * t_new, w), F32),
        compiler_params=_cparams(("parallel", "arbitrary")),
        name="moba_sample",
    )(page_table.reshape(-1), *([cache_k] * pps), *([cache_v] * pps), q, k, v)


def _merge_kernel(orp_ref, ors_ref, oap_ref, oas_ref, gr_ref, ga_ref, wr_ref, wa_ref, o_ref,
                  wrb_ref, wab_ref, *, n_p):
    _cast_once(wr_ref, wrb_ref, 0)
    _cast_once(wa_ref, wab_ref, 0)
    yr = _dot(_two_src_load((orp_ref, ors_ref), n_p).astype(BF16), wrb_ref[...])
    ya = _dot(_two_src_load((oap_ref, oas_ref), n_p).astype(BF16), wab_ref[...])
    o_ref[...] = (gr_ref[...].astype(F32) * yr + ga_ref[...].astype(F32) * ya).astype(BF16)


def _merge(o_r, o_a, gates, w_r, w_a, layer, bm, seq):
    k = o_r[0].shape[1]
    m = gates.shape[0]
    d = w_r.shape[-1]
    n_p = seq // bm
    return pl.pallas_call(
        functools.partial(_merge_kernel, n_p=n_p),
        grid=(m // bm,),
        in_specs=(_two_src_specs(bm, k, n_p) + _two_src_specs(bm, k, n_p) +
                  [pl.BlockSpec((bm, d), lambda i: (i, 0)),
                   pl.BlockSpec((bm, d), lambda i: (i, 1)),
                   _wspec((layer,), (k, d), lambda i: (0, 0), resident=True),
                   _wspec((layer,), (k, d), lambda i: (0, 0), resident=True)]),
        out_specs=pl.BlockSpec((bm, d), lambda i: (i, 0)),
        out_shape=jax.ShapeDtypeStruct((m, d), BF16),
        scratch_shapes=[pltpu.VMEM((k, d), BF16), pltpu.VMEM((k, d), BF16)],
        compiler_params=_cparams(("arbitrary",)),
        name="merge",
    )(*o_r, *o_a, gates, gates, w_r, w_a)


def _cross_head(q, k, v):
    s = _dot_nt(q.astype(BF16), k.astype(BF16))
    p = jnp.exp(s - jnp.max(s, axis=-1, keepdims=True))
    l = jnp.sum(p, axis=-1, keepdims=True)
    return _dot(p.astype(BF16), v.astype(BF16)) / l


def _cross_prompt_kernel(q_ref, k_ref, v_ref, o_ref, *, heads):
    for h in range(heads):
        sl = slice(h * HEAD, (h + 1) * HEAD)
        o_ref[:, sl] = _cross_head(q_ref[:, sl], k_ref[:, sl], v_ref[:, sl])


def _cross_sample_kernel(q_ref, k_ref, v_ref, o_ref, *, heads, spb, t_new):
    for s in range(spb):
        rows = slice(s * t_new, (s + 1) * t_new)
        for h in range(heads):
            sl = slice(h * HEAD, (h + 1) * HEAD)
            o_ref[rows, sl] = _cross_head(q_ref[rows, sl], k_ref[s, :, h, :], v_ref[s, :, h, :])


def _cross_prompt(qc, mk, mv, seq, heads, bq):
    w = qc.shape[1]
    n = mk.shape[0]
    return pl.pallas_call(
        functools.partial(_cross_prompt_kernel, heads=heads),
        grid=(seq // bq,),
        in_specs=[pl.BlockSpec((bq, w), lambda i: (i, 0)),
                  pl.BlockSpec((n, w), lambda i: (0, 0)),
                  pl.BlockSpec((n, w), lambda i: (0, 0))],
        out_specs=pl.BlockSpec((bq, w), lambda i: (i, 0)),
        out_shape=jax.ShapeDtypeStruct((seq, w), F32),
        compiler_params=_cparams(("parallel",)),
        name="cross_prompt",
    )(qc, mk, mv)


def _cross_sample(qc, mem_k, mem_v, layer, seq, t_new):
    depth, batch, n, heads, _ = mem_k.shape
    w = heads * HEAD
    spb = _pick(batch, (4, 2, 1))
    rows = spb * t_new
    assert seq % rows == 0
    r0 = seq // rows
    mem_spec = pl.BlockSpec((None, spb, n, heads, HEAD), lambda b: (layer, b, 0, 0, 0))
    return pl.pallas_call(
        functools.partial(_cross_sample_kernel, heads=heads, spb=spb, t_new=t_new),
        grid=(batch // spb,),
        in_specs=[pl.BlockSpec((rows, w), lambda b: (r0 + b, 0)), mem_spec, mem_spec],
        out_specs=pl.BlockSpec((rows, w), lambda b: (b, 0)),
        out_shape=jax.ShapeDtypeStruct((batch * t_new, w), F32),
        compiler_params=_cparams(("parallel",)),
        name="cross_sample",
    )(qc, mem_k, mem_v)


def _rope_tables(seq, past_len, batch, t_new):
    half = HEAD // 2
    inv_freq = ROPE_THETA ** (-jnp.arange(half, dtype=F32) / half)
    pos = jnp.concatenate([jnp.arange(seq, dtype=jnp.int32),
                           jnp.tile(past_len + jnp.arange(t_new, dtype=jnp.int32), batch)])
    ang = pos.astype(F32)[:, None] * inv_freq[None, :]
    cos = jnp.cos(ang)
    sin = jnp.sin(ang)
    return jnp.concatenate([cos, cos], axis=-1), jnp.concatenate([-sin, sin], axis=-1)


def kernel(x_prompt, x_sample, cache_k, cache_v, state_hgrn, cache_mem_k, cache_mem_v, page_table, mem_prompt,
           norm_pre, norm_post, ffn_gate, ffn_up, ffn_down, w_in, lower_bounds, hgrn_gnorm,
           w_branch_r, w_branch_a, w_out, mem_norm, w_cq, w_ck, w_cv, w_co):
    b_p, seq, d = x_prompt.shape
    batch, t_new, _ = x_sample.shape
    depth, n_pool, page, h_a, _ = cache_k.shape
    h_r = state_hgrn.shape[2]
    h_m = cache_mem_k.shape[3]
    n_mem = cache_mem_k.shape[2]
    n_pages = page_table.shape[1]
    past_len = n_pages * page
    assert b_p == 1 and h_r == 8 and h_a == 8
    assert seq % MOBA_BLOCK == 0 and past_len % MOBA_BLOCK == 0 and t_new <= MOBA_BLOCK
    assert t_new % 8 == 0 and seq % t_new == 0 and MOBA_BLOCK % page == 0
    m_s = batch * t_new
    m = seq + m_s
    w8 = 8 * HEAD

    bm = _pick(m, (768, 512, 384, 256, 128))
    bm_res = _pick(m_s, (256, 128))
    assert seq % bm_res == 0
    bm_ff = _pick(m, (768, 512, 384, 256, 128))
    bn_ff = _pick(ffn_gate.shape[-1], (512, 256))
    hg_c = _pick(seq, (256,))
    pps = _pick(n_pages, (16, 8, 4, 2))

    cos_t, sin_t = _rope_tables(seq, past_len, batch, t_new)
    lower_bounds = lower_bounds.astype(F32)

    x = (x_prompt.reshape(seq, d), x_sample.reshape(m_s, d))
    u = _norm_cast(x[0], x[1], norm_pre[0, 0][None], bm_res)

    outs = {k: [] for k in ("kp", "vp", "sp", "mkp", "mvp", "ks", "vs", "ss")}
    for l in range(depth):
        g_pre = lambda i: norm_pre[l, i][None]
        g_post = lambda i: norm_post[l, i][None]
        g_next_layer = norm_pre[l + 1, 0][None] if l + 1 < depth else None

        mk_p, mv_p = _memkv(mem_prompt.reshape(n_mem, d), mem_norm[l][None], w_ck, w_cv, l)

        h, wd_bf = _ffn_up(u, ffn_gate, ffn_up, ffn_down, (l, 0), bm_ff, bn_ff)
        x, u = _proj_res(h, wd_bf, (), x, g_post(0), g_pre(1), 0.5, bm_res, seq)

        pack = _win_hgrn(u, w_in, lower_bounds, l, bm)
        q_a, = _win_rope(u, w_in, l, cos_t, sin_t, 4, h_a, bm, False)
        k_a, k_bf = _win_rope(u, w_in, l, cos_t, sin_t, 5, h_a, bm, True)
        v, vt = _win_v(u, w_in, l, 6, h_a, bm)
        gates = _win_gates(u, w_in, l, 7, 2 * d // w8, bm)

        o_r_p, s_p = _hgrn_prompt(pack, hgrn_gnorm, l, seq, h_r, hg_c, HGRN_HEADS_PER_STEP)
        o_r_s, s_s = _hgrn_sample(pack, hgrn_gnorm, state_hgrn, l, seq, batch, h_r, t_new)

        o_a_p = _moba_prompt(q_a, k_a, k_bf, vt, seq, h_a)
        o_a_s = _moba_sample(cache_k, cache_v, l, page_table, q_a, k_a, v, seq, t_new, pps)

        merged = _merge((o_r_p, o_r_s), (o_a_p, o_a_s), gates, w_branch_r, w_branch_a, l, bm_res, seq)
        x, u = _proj_res(merged, w_out, (l,), x, g_post(1), g_pre(2), 1.0, bm_res, seq)

        qc = _scaled_proj(u, w_cq, (l,), HEAD ** -0.5, bm)
        oc_p = _cross_prompt(qc, mk_p, mv_p, seq, h_m, _pick(seq, (512, 256)))
        oc_s = _cross_sample(qc, cache_mem_k, cache_mem_v, l, seq, t_new)
        x, u = _proj_res((oc_p, oc_s), w_co, (l,), x, g_post(2), g_pre(3), 1.0, bm_res, seq)

        h, wd_bf = _ffn_up(u, ffn_gate, ffn_up, ffn_down, (l, 1), bm_ff, bn_ff)
        res = _proj_res(h, wd_bf, (), x, g_post(3), g_next_layer, 0.5, bm_res, seq)
        if g_next_layer is None:
            x = tuple(res)
        else:
            x, u = res

        k_all = k_a
        outs["kp"].append(k_all[:seq].reshape(1, seq, h_a, HEAD))
        outs["vp"].append(v[:seq].reshape(1, seq, h_a, HEAD))
        outs["sp"].append(s_p.reshape(1, h_r, HEAD, HEAD))
        outs["mkp"].append(mk_p.reshape(1, n_mem, h_m, HEAD))
        outs["mvp"].append(mv_p.reshape(1, n_mem, h_m, HEAD))
        outs["ks"].append(k_all[seq:].reshape(batch, t_new, h_a, HEAD))
        outs["vs"].append(v[seq:].reshape(batch, t_new, h_a, HEAD))
        outs["ss"].append(s_s)

    st = lambda k: jnp.stack(outs[k])
    return (x[0].reshape(1, seq, d), x[1].reshape(batch, t_new, d),
            st("kp"), st("vp"), st("sp"), st("mkp"), st("mvp"), st("ks"), st("vs"), st("ss"))
```

```python
import functools

import numpy as np
import jax
import jax.numpy as jnp
from jax import lax
from jax.experimental import pallas as pl
from jax.experimental.pallas import tpu as pltpu

F32 = jnp.float32
BF16 = jnp.bfloat16

MOBA_BLOCK = 256
MOBA_TOPK = 3
ROPE_THETA = 10000.0
EPS = 1e-6
HEAD = 128
LANES = 128
NEG = -1e30
LOG2E = 1.4426950408889634
VMEM_LIMIT = 56 * 1024 * 1024
HGRN_HEADS_PER_STEP = 8


def _cparams(sem, vmem=VMEM_LIMIT):
    return pltpu.CompilerParams(dimension_semantics=sem, vmem_limit_bytes=vmem)


def _pick(n, cands):
    for c in cands:
        if n % c == 0:
            return c
    raise ValueError(f"no tile in {cands} divides {n}")


def _wspec(lead, shape, index_map, resident=False):
    full_shape = (None,) * len(lead) + tuple(shape)
    full_map = lambda *a: tuple(lead) + tuple(index_map(*a))
    if resident:
        return pl.BlockSpec(full_shape, full_map, pipeline_mode=pl.Buffered(1))
    return pl.BlockSpec(full_shape, full_map)


def _two_src_specs(bm, k, n_p):
    return [pl.BlockSpec((bm, k), lambda i: (jnp.minimum(i, n_p - 1), 0)),
            pl.BlockSpec((bm, k), lambda i: (jnp.maximum(i - n_p, 0), 0))]


def _two_src_load(refs, n_p):
    return jnp.where(pl.program_id(0) < n_p, refs[0][...], refs[1][...])


def _dot(a, b):
    return jnp.dot(a, b, preferred_element_type=F32)


def _dot_nt(a, b):
    return lax.dot_general(a, b, (((1,), (1,)), ((), ())), preferred_element_type=F32)


def _split2(a):
    hi = a.astype(BF16)
    lo = (a - hi.astype(F32)).astype(BF16)
    return hi, lo


def _dot3_nt(a, b):
    ah, al = _split2(a)
    bh, bl = _split2(b)
    return _dot_nt(ah, bh) + _dot_nt(ah, bl) + _dot_nt(al, bh)


def _dot3(a, b):
    ah, al = _split2(a)
    bh, bl = _split2(b)
    return _dot(ah, bh) + _dot(ah, bl) + _dot(al, bh)


def _sigmoid(x):
    return 1.0 / (1.0 + jnp.exp(-x))


def _silu(x):
    return x * _sigmoid(x)


def _rms(y, g):
    ms = jnp.mean(y * y, axis=-1, keepdims=True)
    return y * lax.rsqrt(ms + EPS) * g


def _cast_once(w_ref, wb_ref, axis):
    @pl.when(pl.program_id(axis) == 0)
    def _():
        wb_ref[...] = w_ref[...].astype(BF16)


def _norm_kernel(xp_ref, xs_ref, g_ref, u_ref, *, n_p):
    u_ref[...] = _rms(_two_src_load((xp_ref, xs_ref), n_p), g_ref[...]).astype(BF16)


def _norm_cast(x_p, x_s, g, bm):
    seq, d = x_p.shape
    m = seq + x_s.shape[0]
    n_p = seq // bm
    return pl.pallas_call(
        functools.partial(_norm_kernel, n_p=n_p),
        grid=(m // bm,),
        in_specs=_two_src_specs(bm, d, n_p) + [pl.BlockSpec((1, d), lambda i: (0, 0))],
        out_specs=pl.BlockSpec((bm, d), lambda i: (i, 0)),
        out_shape=jax.ShapeDtypeStruct((m, d), BF16),
        compiler_params=_cparams(("parallel",)),
        name="norm_cast",
    )(x_p, x_s, g)


def _ffn_up_kernel(u_ref, wg_ref, wu_ref, h_ref, wgb_ref, wub_ref):
    _cast_once(wg_ref, wgb_ref, 1)
    _cast_once(wu_ref, wub_ref, 1)
    u = u_ref[...]
    h_ref[...] = (_silu(_dot(u, wgb_ref[...])) * _dot(u, wub_ref[...])).astype(BF16)


def _ffn_up(u, wg, wu, lead, bm, bn):
    m, d = u.shape
    f = wg.shape[-1]
    return pl.pallas_call(
        _ffn_up_kernel,
        grid=(f // bn, m // bm),
        in_specs=[pl.BlockSpec((bm, d), lambda n, i: (i, 0)),
                  _wspec(lead, (d, bn), lambda n, i: (0, n)),
                  _wspec(lead, (d, bn), lambda n, i: (0, n))],
        out_specs=pl.BlockSpec((bm, bn), lambda n, i: (i, n)),
        out_shape=jax.ShapeDtypeStruct((m, f), BF16),
        scratch_shapes=[pltpu.VMEM((d, bn), BF16), pltpu.VMEM((d, bn), BF16)],
        compiler_params=_cparams(("arbitrary", "arbitrary")),
        name="ffn_up",
    )(u, wg, wu)


def _proj_res_kernel(*refs, scale, a2, x2, split, n_p, cast_w):
    refs = list(refs)
    a_refs = [refs.pop(0) for _ in range(2 if a2 else 1)]
    w_ref = refs.pop(0)
    if cast_w:
        wb_ref = refs.pop()
        _cast_once(w_ref, wb_ref, 0)
        w_ref = wb_ref
    x_refs = [refs.pop(0) for _ in range(2 if x2 else 1)]
    gp_ref = refs.pop(0)
    a = _two_src_load(a_refs, n_p) if a2 else a_refs[0][...]
    x = _two_src_load(x_refs, n_p) if x2 else x_refs[0][...]
    xn = x + scale * _rms(_dot(a.astype(BF16), w_ref[...]), gp_ref[...])
    if split:
        xop_ref, xos_ref = refs
        is_p = pl.program_id(0) < n_p

        @pl.when(is_p)
        def _():
            xop_ref[...] = xn

        @pl.when(jnp.logical_not(is_p))
        def _():
            xos_ref[...] = xn
    else:
        gn_ref, xo_ref, uo_ref = refs
        xo_ref[...] = xn
        uo_ref[...] = _rms(xn, gn_ref[...]).astype(BF16)


def _proj_res(a, w, lead, x, g_post, g_next, scale, bm, seq):
    a2, x2, split = isinstance(a, tuple), isinstance(x, tuple), g_next is None
    cast_w = w.dtype != BF16
    k, d = w.shape[-2:]
    n_p = seq // bm
    m = (x[0].shape[0] + x[1].shape[0]) if x2 else x.shape[0]
    row = lambda c: pl.BlockSpec((bm, c), lambda i: (i, 0))
    vec = pl.BlockSpec((1, d), lambda i: (0, 0))
    in_specs = (_two_src_specs(bm, k, n_p) if a2 else [row(k)])
    in_specs += [_wspec(lead, (k, d), lambda i: (0, 0), resident=True)]
    in_specs += (_two_src_specs(bm, d, n_p) if x2 else [row(d)]) + [vec]
    args = (list(a) if a2 else [a]) + [w] + (list(x) if x2 else [x]) + [g_post]
    if split:
        out_specs = _two_src_specs(bm, d, n_p)
        out_shape = [jax.ShapeDtypeStruct((seq, d), F32), jax.ShapeDtypeStruct((m - seq, d), F32)]
    else:
        in_specs += [vec]
        args += [g_next]
        out_specs = [row(d), row(d)]
        out_shape = [jax.ShapeDtypeStruct((m, d), F32), jax.ShapeDtypeStruct((m, d), BF16)]
    return pl.pallas_call(
        functools.partial(_proj_res_kernel, scale=scale, a2=a2, x2=x2, split=split, n_p=n_p, cast_w=cast_w),
        grid=(m // bm,),
        in_specs=in_specs,
        out_specs=out_specs,
        out_shape=out_shape,
        scratch_shapes=[pltpu.VMEM((k, d), BF16)] if cast_w else [],
        compiler_params=_cparams(("arbitrary",)),
        name="proj_res",
    )(*args)


def _win_hgrn_kernel(u_ref, w_ref, lb_ref, o_ref, wb_ref, *, layer):
    _cast_once(w_ref, wb_ref, 1)
    n = pl.program_id(0)
    acc = _dot(u_ref[...], wb_ref[...])

    @pl.when(n == 0)
    def _():
        o_ref[...] = _silu(acc) * (HEAD ** -0.5)

    @pl.when(n == 1)
    def _():
        lbs = lb_ref[...]
        e = jnp.exp(lbs - jnp.max(lbs, axis=0, keepdims=True))
        sm = e / jnp.sum(e, axis=0, keepdims=True)
        cum = sm[0:1, :]
        for i in range(1, layer + 1):
            cum = cum + sm[i:i + 1, :]
        lb = cum - sm[0:1, :]
        ls = jnp.minimum(acc, 0.0) - jnp.log1p(jnp.exp(-jnp.abs(acc)))
        a = jnp.log(lb)
        b = jnp.log1p(-lb) + ls
        o_ref[...] = jnp.maximum(a, b) + jnp.log1p(jnp.exp(-jnp.abs(a - b)))

    @pl.when(n == 2)
    def _():
        o_ref[...] = acc

    @pl.when(n == 3)
    def _():
        o_ref[...] = _silu(acc)


def _win_hgrn(u, w_in, lower_bounds, layer, bm):
    m, d = u.shape
    w = 8 * HEAD
    depth = lower_bounds.shape[0]
    return pl.pallas_call(
        functools.partial(_win_hgrn_kernel, layer=layer),
        grid=(4, m // bm),
        in_specs=[pl.BlockSpec((bm, d), lambda n, i: (i, 0)),
                  _wspec((layer,), (d, w), lambda n, i: (0, n)),
                  pl.BlockSpec((depth, w), lambda n, i: (0, 0))],
        out_specs=pl.BlockSpec((bm, w), lambda n, i: (i, n)),
        out_shape=jax.ShapeDtypeStruct((m, 4 * w), F32),
        scratch_shapes=[pltpu.VMEM((d, w), BF16)],
        compiler_params=_cparams(("arbitrary", "arbitrary")),
        name="win_hgrn",
    )(u, w_in, lower_bounds)


def _win_rope_kernel(u_ref, w_ref, cos_ref, sin_ref, o_ref, *rest, heads):
    wb_ref = rest[-1]
    _cast_once(w_ref, wb_ref, 0)
    acc = _dot(u_ref[...], wb_ref[...])
    c = cos_ref[...]
    s = sin_ref[...]
    for h in range(heads):
        sl = slice(h * HEAD, (h + 1) * HEAD)
        xh = acc[:, sl]
        r = xh * c + pltpu.roll(xh, HEAD // 2, 1) * s
        o_ref[:, sl] = r
        if len(rest) == 2:
            rest[0][:, sl] = r.astype(BF16)


def _win_rope(u, w_in, layer, cos_t, sin_t, col_blk, heads, bm, with_bf16):
    m, d = u.shape
    w = heads * HEAD
    row = pl.BlockSpec((bm, w), lambda i: (i, 0))
    return pl.pallas_call(
        functools.partial(_win_rope_kernel, heads=heads),
        grid=(m // bm,),
        in_specs=[pl.BlockSpec((bm, d), lambda i: (i, 0)),
                  _wspec((layer,), (d, w), lambda i: (0, col_blk), resident=True),
                  pl.BlockSpec((bm, HEAD), lambda i: (i, 0)),
                  pl.BlockSpec((bm, HEAD), lambda i: (i, 0))],
        out_specs=[row, row] if with_bf16 else [row],
        out_shape=([jax.ShapeDtypeStruct((m, w), F32)] +
                   ([jax.ShapeDtypeStruct((m, w), BF16)] if with_bf16 else [])),
        scratch_shapes=[pltpu.VMEM((d, w), BF16)],
        compiler_params=_cparams(("arbitrary",)),
        name="win_rope",
    )(u, w_in, cos_t, sin_t)


def _win_v_kernel(u_ref, w_ref, v_ref, vt_ref, wb_ref):
    _cast_once(w_ref, wb_ref, 0)
    acc = _dot(u_ref[...], wb_ref[...])
    v_ref[...] = acc
    vt_ref[...] = acc.T.astype(BF16)


def _win_v(u, w_in, layer, col_blk, heads, bm):
    m, d = u.shape
    w = heads * HEAD
    return pl.pallas_call(
        _win_v_kernel,
        grid=(m // bm,),
        in_specs=[pl.BlockSpec((bm, d), lambda i: (i, 0)),
                  _wspec((layer,), (d, w), lambda i: (0, col_blk), resident=True)],
        out_specs=[pl.BlockSpec((bm, w), lambda i: (i, 0)), pl.BlockSpec((w, bm), lambda i: (0, i))],
        out_shape=[jax.ShapeDtypeStruct((m, w), F32), jax.ShapeDtypeStruct((w, m), BF16)],
        scratch_shapes=[pltpu.VMEM((d, w), BF16)],
        compiler_params=_cparams(("arbitrary",)),
        name="win_v",
    )(u, w_in)


def _win_gate_kernel(u_ref, w_ref, o_ref, wb_ref):
    _cast_once(w_ref, wb_ref, 1)
    o_ref[...] = _sigmoid(_dot(u_ref[...], wb_ref[...])).astype(BF16)


def _win_gates(u, w_in, layer, col_blk, n_blk, bm):
    m, d = u.shape
    w = 8 * HEAD
    return pl.pallas_call(
        _win_gate_kernel,
        grid=(n_blk, m // bm),
        in_specs=[pl.BlockSpec((bm, d), lambda n, i: (i, 0)),
                  _wspec((layer,), (d, w), lambda n, i: (0, col_blk + n))],
        out_specs=pl.BlockSpec((bm, w), lambda n, i: (i, n)),
        out_shape=jax.ShapeDtypeStruct((m, n_blk * w), BF16),
        scratch_shapes=[pltpu.VMEM((d, w), BF16)],
        compiler_params=_cparams(("arbitrary", "arbitrary")),
        name="win_gates",
    )(u, w_in)


def _scaled_proj_kernel(u_ref, w_ref, o_ref, wb_ref, *, scale):
    _cast_once(w_ref, wb_ref, 0)
    o_ref[...] = _dot(u_ref[...], wb_ref[...]) * scale


def _scaled_proj(u, w, lead, scale, bm):
    m, d = u.shape
    n = w.shape[-1]
    return pl.pallas_call(
        functools.partial(_scaled_proj_kernel, scale=scale),
        grid=(m // bm,),
        in_specs=[pl.BlockSpec((bm, d), lambda i: (i, 0)),
                  _wspec(lead, (d, n), lambda i: (0, 0), resident=True)],
        out_specs=pl.BlockSpec((bm, n), lambda i: (i, 0)),
        out_shape=jax.ShapeDtypeStruct((m, n), F32),
        scratch_shapes=[pltpu.VMEM((d, n), BF16)],
        compiler_params=_cparams(("arbitrary",)),
        name="scaled_proj",
    )(u, w)


def _memkv_kernel(mem_ref, g_ref, wk_ref, wv_ref, k_ref, v_ref):
    mn = _rms(mem_ref[...], g_ref[...]).astype(BF16)
    k_ref[...] = _dot(mn, wk_ref[...].astype(BF16))
    v_ref[...] = _dot(mn, wv_ref[...].astype(BF16))


def _memkv(mem, g, wk, wv, layer):
    n, d = mem.shape
    w = wk.shape[-1]
    return pl.pallas_call(
        _memkv_kernel,
        grid=(1,),
        in_specs=[pl.BlockSpec((n, d), lambda i: (0, 0)),
                  pl.BlockSpec((1, d), lambda i: (0, 0)),
                  _wspec((layer,), (d, w), lambda i: (0, 0)),
                  _wspec((layer,), (d, w), lambda i: (0, 0))],
        out_specs=[pl.BlockSpec((n, w), lambda i: (0, 0)), pl.BlockSpec((n, w), lambda i: (0, 0))],
        out_shape=[jax.ShapeDtypeStruct((n, w), F32), jax.ShapeDtypeStruct((n, w), F32)],
        compiler_params=_cparams(("arbitrary",)),
        name="memkv",
    )(mem, g, wk, wv)


def _cumsum_rows(g, c):
    row = lax.broadcasted_iota(jnp.int32, g.shape, 0)
    out = g
    sh = 1
    while sh < c:
        out = out + jnp.where(row >= sh, pltpu.roll(out, sh, 0), 0.0)
        sh *= 2
    return out


def _hgrn_chunk(q, g, v, st, c):
    g = g * LOG2E
    k = 1.0 - jnp.exp2(g)
    gc = _cumsum_rows(g, c)
    row = lax.broadcasted_iota(jnp.int32, (c, HEAD), 0)
    ra = lax.broadcasted_iota(jnp.int32, (c, c), 0)
    ca = lax.broadcasted_iota(jnp.int32, (c, c), 1)
    x = jnp.bitwise_xor(ra, ca)

    a = jnp.zeros((c, c), F32)
    b = c // 2
    while b >= 1:
        upper = jnp.bitwise_and(row, b) != 0
        if b >= 4:
            nb = c // (2 * b)
            gcb = gc.reshape(nb, 2 * b, HEAD)
            r = gcb[:, b - 1:b, :]
            up3 = upper.reshape(nb, 2 * b, HEAD)
            dlt = jnp.where(up3, gcb - r, r - gcb).reshape(c, HEAD)
        elif b == 2:
            t4 = jnp.bitwise_and(row, 3)
            g_next = pltpu.roll(g, c - 1, 0)
            g_prev = pltpu.roll(g, 1, 0)
            dlt = jnp.where(t4 == 0, g_next, jnp.where(t4 == 1, 0.0, jnp.where(t4 == 2, g, g + g_prev)))
        else:
            dlt = jnp.where(upper, g, 0.0)
        z = (jnp.where(upper, q, k) * jnp.exp2(dlt)).astype(BF16)
        p = _dot_nt(z, z)
        a = p if b == c // 2 else jnp.where(x < 2 * b, p, a)
        b //= 2
    a = jnp.where(ra > ca, a, 0.0)

    vb = v.astype(BF16)
    o = _dot(a.astype(BF16), vb)
    o = o + jnp.sum(q * k, axis=-1, keepdims=True) * v
    o = o + _dot_nt((q * jnp.exp2(gc)).astype(BF16), st.astype(BF16))
    g_end = gc[c - 1:c, :]
    khat = (k * jnp.exp2(g_end - gc)).astype(BF16)
    st_new = st * jnp.exp2(g_end) + lax.dot_general(vb, khat, (((0,), (0,)), ((), ())),
                                                    preferred_element_type=F32)
    return o, st_new


def _hgrn_out(o, gn, og):
    return _rms(o, gn) * og


def _hgrn_prompt_kernel(q_ref, g_ref, v_ref, og_ref, gn_ref, o_ref, s_ref, st_ref, *, c, hpb):
    hg = pl.program_id(0)
    ci = pl.program_id(1)

    @pl.when(ci == 0)
    def _():
        st_ref[...] = jnp.zeros_like(st_ref)

    gn_all = gn_ref[...]
    hrow = lax.broadcasted_iota(jnp.int32, gn_all.shape, 0)
    sls = [slice(j * HEAD, (j + 1) * HEAD) for j in range(hpb)]
    res = [_hgrn_chunk(q_ref[:, sl], g_ref[:, sl], v_ref[:, sl], st_ref[j], c) for j, sl in enumerate(sls)]
    for j, (o, st_new) in enumerate(res):
        gn = jnp.sum(jnp.where(hrow == hg * hpb + j, gn_all, 0.0), axis=0, keepdims=True)
        st_ref[j] = st_new
        o_ref[:, sls[j]] = _hgrn_out(o, gn, og_ref[:, sls[j]])

    @pl.when(ci == pl.num_programs(1) - 1)
    def _():
        for j, (_, st_new) in enumerate(res):
            s_ref[j] = st_new.T


def _hgrn_prompt(pack, gnorm, layer, seq, heads, c, hpb):
    n_hg = heads // hpb
    blk = lambda off: pl.BlockSpec((c, hpb * HEAD), lambda h, i: (i, off * n_hg + h))
    return pl.pallas_call(
        functools.partial(_hgrn_prompt_kernel, c=c, hpb=hpb),
        grid=(n_hg, seq // c),
        in_specs=[blk(0), blk(1), blk(2), blk(3),
                  _wspec((layer,), (heads, HEAD), lambda h, i: (0, 0))],
        out_specs=[pl.BlockSpec((c, hpb * HEAD), lambda h, i: (i, h)),
                   pl.BlockSpec((hpb, HEAD, HEAD), lambda h, i: (h, 0, 0))],
        out_shape=[jax.ShapeDtypeStruct((seq, heads * HEAD), F32),
                   jax.ShapeDtypeStruct((heads, HEAD, HEAD), F32)],
        scratch_shapes=[pltpu.VMEM((hpb, HEAD, HEAD), F32)],
        compiler_params=_cparams(("parallel", "arbitrary")),
        name="hgrn_prompt",
    )(pack, pack, pack, pack, gnorm)


def _hgrn_sample_kernel(q_ref, g_ref, v_ref, og_ref, gn_ref, s0_ref, o_ref, s_ref, *, c, heads):
    sls = [slice(h * HEAD, (h + 1) * HEAD) for h in range(heads)]
    res = [_hgrn_chunk(q_ref[:, sl], g_ref[:, sl], v_ref[:, sl], s0_ref[h].T, c) for h, sl in enumerate(sls)]
    for h, (o, st_new) in enumerate(res):
        o_ref[:, sls[h]] = _hgrn_out(o, gn_ref[h:h + 1, :], og_ref[:, sls[h]])
        s_ref[h] = st_new.T


def _hgrn_sample(pack, gnorm, s0, layer, seq, batch, heads, c):
    w = heads * HEAD
    r0 = seq // c
    blk = lambda off: pl.BlockSpec((c, w), lambda b: (r0 + b, off))
    return pl.pallas_call(
        functools.partial(_hgrn_sample_kernel, c=c, heads=heads),
        grid=(batch,),
        in_specs=[blk(0), blk(1), blk(2), blk(3),
                  _wspec((layer,), (heads, HEAD), lambda b: (0, 0)),
                  _wspec((layer,), (None, heads, HEAD, HEAD), lambda b: (b, 0, 0, 0))],
        out_specs=[pl.BlockSpec((c, w), lambda b: (b, 0)),
                   pl.BlockSpec((None, heads, HEAD, HEAD), lambda b: (b, 0, 0, 0))],
        out_shape=[jax.ShapeDtypeStruct((batch * c, w), F32),
                   jax.ShapeDtypeStruct(s0.shape[1:], F32)],
        compiler_params=_cparams(("parallel",)),
        name="hgrn_sample",
    )(pack, pack, pack, pack, gnorm, s0)


def _topk_mask(score, valid, axis):
    idx = lax.broadcasted_iota(jnp.int32, score.shape, axis)
    big = jnp.int32(2 ** 30)
    sg = jnp.where(valid, score, -jnp.inf)
    sel = jnp.zeros(score.shape, F32)
    for _ in range(MOBA_TOPK):
        mx = jnp.max(sg, axis=axis, keepdims=True)
        first = jnp.min(jnp.where(sg == mx, idx, big), axis=axis, keepdims=True)
        pick = jnp.logical_and(idx == first, mx > -jnp.inf)
        sel = jnp.where(pick, 1.0, sel)
        sg = jnp.where(pick, -jnp.inf, sg)
    return sel


def _moba_prompt_kernel(it_ref, jt_ref, fl_ref, q_ref, ko_ref, vto_ref, kp_ref, vtp_ref, o_ref,
                        qs_ref, m_ref, l_ref, acc_ref, sel_ref, km_ref, *, heads, nbp):
    s = pl.program_id(0)
    i = it_ref[s]
    jp = jt_ref[s]
    own = jnp.bitwise_and(fl_ref[s], 1) == 1
    last = jnp.bitwise_and(fl_ref[s], 2) == 2
    blk = MOBA_BLOCK
    hs = [slice(h * HEAD, (h + 1) * HEAD) for h in range(heads)]

    @pl.when(s == 0)
    def _():
        km_ref[...] = jnp.zeros_like(km_ref)

    @pl.when(own)
    def _():
        k = ko_ref[...]
        kb = k.astype(BF16)
        km_ref[pl.ds(i, 1), :] = jnp.mean(k, axis=0, keepdims=True)
        qt = q_ref[...].T
        qs_ref[...] = (qt * (HEAD ** -0.5 * LOG2E)).astype(BF16)
        key = lax.broadcasted_iota(jnp.int32, (blk, blk), 0)
        qry = lax.broadcasted_iota(jnp.int32, (blk, blk), 1)
        causal = jnp.where(key <= qry, 0.0, NEG)
        bidx = lax.broadcasted_iota(jnp.int32, (nbp, blk), 0)
        km = km_ref[...]
        qsb = qs_ref[...]
        gates = [_dot3(km[:, sl], qt[sl, :]) for sl in hs]
        sts = [_dot(kb[:, sl], qsb[sl, :]) for sl in hs]
        m_rows, l_rows, ps = [], [], []
        for h in range(heads):
            st = sts[h] + causal
            m = jnp.max(st, axis=0, keepdims=True)
            p = jnp.exp2(st - m)
            m_rows.append(m)
            l_rows.append(jnp.sum(p, axis=0, keepdims=True))
            ps.append(p.astype(BF16))
        for h, sl in enumerate(hs):
            acc_ref[sl, :] = _dot(vto_ref[sl, :], ps[h])
            sel_ref[h] = _topk_mask(gates[h], bidx < i, 0)
        m_ref[...] = jnp.concatenate(m_rows, axis=0)
        l_ref[...] = jnp.concatenate(l_rows, axis=0)

    @pl.when(jnp.logical_not(own))
    def _():
        kb = kp_ref[...]
        m_all = m_ref[...]
        l_all = l_ref[...]
        sel_a = sel_ref[:, pl.ds(2 * jp, 1), :]
        sel_b = sel_ref[:, pl.ds(2 * jp + 1, 1), :]
        sts = [_dot(kb[:, sl], qs_ref[sl, :]) for sl in hs]
        m_rows, l_rows, ps, alphas = [], [], [], []
        for h in range(heads):
            st_a = sts[h][:blk] + (sel_a[h] - 1.0) * (-NEG)
            st_b = sts[h][blk:] + (sel_b[h] - 1.0) * (-NEG)
            m_old = m_all[h:h + 1, :]
            m_new = jnp.maximum(m_old, jnp.maximum(jnp.max(st_a, axis=0, keepdims=True),
                                                   jnp.max(st_b, axis=0, keepdims=True)))
            alpha = jnp.exp2(m_old - m_new)
            p_a = jnp.exp2(st_a - m_new)
            p_b = jnp.exp2(st_b - m_new)
            m_rows.append(m_new)
            l_rows.append(alpha * l_all[h:h + 1, :] + jnp.sum(p_a, axis=0, keepdims=True)
                          + jnp.sum(p_b, axis=0, keepdims=True))
            ps.append(jnp.concatenate([p_a.astype(BF16), p_b.astype(BF16)], axis=0))
            alphas.append(alpha)
        for h, sl in enumerate(hs):
            acc_ref[sl, :] = alphas[h] * acc_ref[sl, :] + _dot(vtp_ref[sl, :], ps[h])
        m_ref[...] = jnp.concatenate(m_rows, axis=0)
        l_ref[...] = jnp.concatenate(l_rows, axis=0)

    @pl.when(last)
    def _():
        for h, sl in enumerate(hs):
            o_ref[:, sl] = (acc_ref[sl, :] / l_ref[h:h + 1, :]).T


def _moba_prompt(q, k, k_bf, vt, seq, heads):
    w = heads * HEAD
    blk = MOBA_BLOCK
    nb = seq // blk
    assert nb % 2 == 0
    nbp = -(-nb // 8) * 8
    it, jt, fl = [], [], []
    for i in range(nb):
        it.append(i)
        jt.append(0)
        fl.append(1)
        for jp in range((i + 1) // 2):
            it.append(i)
            jt.append(jp)
            fl.append(0)
        fl[-1] += 2
    tabs = [jnp.asarray(np.asarray(t, np.int32)) for t in (it, jt, fl)]
    grid_spec = pltpu.PrefetchScalarGridSpec(
        num_scalar_prefetch=3,
        grid=(len(it),),
        in_specs=[pl.BlockSpec((blk, w), lambda s, it, jt, fl: (it[s], 0)),
                  pl.BlockSpec((blk, w), lambda s, it, jt, fl: (it[s], 0)),
                  pl.BlockSpec((w, blk), lambda s, it, jt, fl: (0, it[s])),
                  pl.BlockSpec((2 * blk, w), lambda s, it, jt, fl: (jt[s], 0)),
                  pl.BlockSpec((w, 2 * blk), lambda s, it, jt, fl: (0, jt[s]))],
        out_specs=pl.BlockSpec((blk, w), lambda s, it, jt, fl: (it[s], 0)),
        scratch_shapes=[pltpu.VMEM((w, blk), BF16),
                        pltpu.VMEM((heads, blk), F32),
                        pltpu.VMEM((heads, blk), F32),
                        pltpu.VMEM((w, blk), F32),
                        pltpu.VMEM((heads, nbp, blk), F32),
                        pltpu.VMEM((nbp, w), F32)],
    )
    return pl.pallas_call(
        functools.partial(_moba_prompt_kernel, heads=heads, nbp=nbp),
        grid_spec=grid_spec,
        out_shape=jax.ShapeDtypeStruct((seq, w), F32),
        compiler_params=_cparams(("arbitrary",)),
        name="moba_prompt",
    )(*tabs, q, k, vt, k_bf, vt)


def _moba_sample_kernel(pt_ref, *refs, heads, t_new, pps, nb, page):
    del pt_ref
    k_refs = refs[:pps]
    v_refs = refs[pps:2 * pps]
    q_ref, kn_ref, vn_ref, o_ref, m_ref, l_ref, ob_ref, km_ref = refs[2 * pps:]
    step = pl.program_id(1)
    ppb = MOBA_BLOCK // page
    bps = pps // ppb
    rows = heads * t_new
    cols = ppb * page * heads

    q = q_ref[...]
    q_all = jnp.concatenate([q[:, h * HEAD:(h + 1) * HEAD] for h in range(heads)], axis=0)
    qs = (q_all * (HEAD ** -0.5 * LOG2E)).astype(BF16)

    rh = lax.broadcasted_iota(jnp.int32, (rows, cols), 0) // t_new
    ch = jnp.bitwise_and(lax.broadcasted_iota(jnp.int32, (rows, cols), 1), heads - 1)
    head_bias = jnp.where(rh == ch, 0.0, NEG)

    blocks = [range(bi * ppb, (bi + 1) * ppb) for bi in range(bps)]
    scs = []
    for pages in blocks:
        k2 = jnp.concatenate([k_refs[p][...].reshape(page * heads, HEAD) for p in pages], axis=0)
        scs.append(_dot_nt(qs, k2.astype(BF16)))
    ms, ls, ps = [], [], []
    for sc in scs:
        sc = sc + head_bias
        m = sc.max(axis=-1, keepdims=True)
        p_exp = jnp.exp2(sc - m)
        ms.append(m)
        ls.append(p_exp.sum(axis=-1, keepdims=True))
        ps.append(p_exp.astype(BF16))
    for bi, pages in enumerate(blocks):
        jb = step * bps + bi
        v2 = jnp.concatenate([v_refs[p][...].reshape(page * heads, HEAD) for p in pages], axis=0)
        ob_ref[jb] = _dot(ps[bi], v2.astype(BF16))
        m_ref[jb] = jnp.broadcast_to(ms[bi], (rows, LANES))
        l_ref[jb] = jnp.broadcast_to(ls[bi], (rows, LANES))
        ksum = k_refs[pages[0]][...].sum(axis=0)
        for p in pages[1:]:
            ksum = ksum + k_refs[p][...].sum(axis=0)
        km_ref[jb] = ksum * (1.0 / MOBA_BLOCK)

    @pl.when(step == pl.num_programs(1) - 1)
    def _():
        tq = lax.broadcasted_iota(jnp.int32, (t_new, LANES), 0)
        kk = lax.broadcasted_iota(jnp.int32, (t_new, LANES), 1)
        causal = jnp.where(kk <= tq, 0.0, NEG)
        zpad = jnp.zeros((LANES - t_new, HEAD), F32)
        m_o, l_o, o_o = [], [], []
        for h in range(heads):
            sl = slice(h * HEAD, (h + 1) * HEAD)
            k_own = jnp.concatenate([kn_ref[:, sl], zpad], axis=0).astype(BF16)
            v_own = jnp.concatenate([vn_ref[:, sl], zpad], axis=0).astype(BF16)
            s_o = _dot_nt(qs[h * t_new:(h + 1) * t_new, :], k_own) + causal
            mh = s_o.max(axis=-1, keepdims=True)
            ph = jnp.exp2(s_o - mh)
            m_o.append(mh)
            l_o.append(ph.sum(axis=-1, keepdims=True))
            o_o.append(_dot(ph.astype(BF16), v_own))
        m_o, l_o, o_o = (jnp.concatenate(x, axis=0) for x in (m_o, l_o, o_o))
        gate = _dot3_nt(q_all, km_ref[...].reshape(nb * heads, HEAD))
        gr = lax.broadcasted_iota(jnp.int32, (rows, nb * heads), 0) // t_new
        gc = jnp.bitwise_and(lax.broadcasted_iota(jnp.int32, (rows, nb * heads), 1), heads - 1)
        sel = _topk_mask(gate, gr == gc, 1)
        picked = [sel[:, jb * heads:(jb + 1) * heads].max(axis=-1, keepdims=True) > 0.5 for jb in range(nb)]
        m_all = m_o
        for jb in range(nb):
            m_all = jnp.maximum(m_all, jnp.where(picked[jb], m_ref[jb][:, 0:1], NEG))
        w_o = jnp.exp2(m_o - m_all)
        l_tot = w_o * l_o
        o_tot = w_o * o_o
        for jb in range(nb):
            wj = jnp.where(picked[jb], jnp.exp2(jnp.where(picked[jb], m_ref[jb][:, 0:1], NEG) - m_all), 0.0)
            l_tot = l_tot + wj * l_ref[jb][:, 0:1]
            o_tot = o_tot + wj * ob_ref[jb]
        res = o_tot / l_tot
        for h in range(heads):
            o_ref[:, h * HEAD:(h + 1) * HEAD] = res[h * t_new:(h + 1) * t_new, :]


def _moba_sample(cache_k, cache_v, layer, page_table, q, k, v, seq, t_new, pps):
    depth, n_pool, page, heads, _ = cache_k.shape
    batch, n_pages = page_table.shape
    assert heads & (heads - 1) == 0
    w = heads * HEAD
    nb = n_pages * page // MOBA_BLOCK
    n_steps = n_pages // pps
    r0 = seq // t_new
    rows = heads * t_new

    def page_spec(p):
        return pl.BlockSpec((None, None, page, heads, HEAD),
                            lambda b, t, pt: (layer, pt[b * n_pages + t * pps + p], 0, 0, 0))

    grid_spec = pltpu.PrefetchScalarGridSpec(
        num_scalar_prefetch=1,
        grid=(batch, n_steps),
        in_specs=([page_spec(p) for p in range(pps)] + [page_spec(p) for p in range(pps)] +
                  [pl.BlockSpec((t_new, w), lambda b, t, pt: (r0 + b, 0)),
                   pl.BlockSpec((t_new, w), lambda b, t, pt: (r0 + b, 0)),
                   pl.BlockSpec((t_new, w), lambda b, t, pt: (r0 + b, 0))]),
        out_specs=pl.BlockSpec((t_new, w), lambda b, t, pt: (b, 0)),
        scratch_shapes=[pltpu.VMEM((nb, rows, LANES), F32),
                        pltpu.VMEM((nb, rows, LANES), F32),
                        pltpu.VMEM((nb, rows, HEAD), F32),
                        pltpu.VMEM((nb, heads, HEAD), F32)],
    )
    return pl.pallas_call(
        functools.partial(_moba_sample_kernel, heads=heads, t_new=t_new, pps=pps, nb=nb, page=page),
        grid_spec=grid_spec,
        out_shape=jax.ShapeDtypeStruct((batch * t_new, w), F32),
        compiler_params=_cparams(("parallel", "arbitrary")),
        name="moba_sample",
    )(page_table.reshape(-1), *([cache_k] * pps), *([cache_v] * pps), q, k, v)


def _merge_kernel(orp_ref, ors_ref, oap_ref, oas_ref, gr_ref, ga_ref, wr_ref, wa_ref, o_ref,
                  wrb_ref, wab_ref, *, n_p):
    _cast_once(wr_ref, wrb_ref, 0)
    _cast_once(wa_ref, wab_ref, 0)
    yr = _dot(_two_src_load((orp_ref, ors_ref), n_p).astype(BF16), wrb_ref[...])
    ya = _dot(_two_src_load((oap_ref, oas_ref), n_p).astype(BF16), wab_ref[...])
    o_ref[...] = (gr_ref[...].astype(F32) * yr + ga_ref[...].astype(F32) * ya).astype(BF16)


def _merge(o_r, o_a, gates, w_r, w_a, layer, bm, seq):
    k = o_r[0].shape[1]
    m = gates.shape[0]
    d = w_r.shape[-1]
    n_p = seq // bm
    return pl.pallas_call(
        functools.partial(_merge_kernel, n_p=n_p),
        grid=(m // bm,),
        in_specs=(_two_src_specs(bm, k, n_p) + _two_src_specs(bm, k, n_p) +
                  [pl.BlockSpec((bm, d), lambda i: (i, 0)),
                   pl.BlockSpec((bm, d), lambda i: (i, 1)),
                   _wspec((layer,), (k, d), lambda i: (0, 0), resident=True),
                   _wspec((layer,), (k, d), lambda i: (0, 0), resident=True)]),
        out_specs=pl.BlockSpec((bm, d), lambda i: (i, 0)),
        out_shape=jax.ShapeDtypeStruct((m, d), BF16),
        scratch_shapes=[pltpu.VMEM((k, d), BF16), pltpu.VMEM((k, d), BF16)],
        compiler_params=_cparams(("arbitrary",)),
        name="merge",
    )(*o_r, *o_a, gates, gates, w_r, w_a)


def _cross_head(q, k, v):
    s = _dot_nt(q.astype(BF16), k.astype(BF16))
    p = jnp.exp(s - jnp.max(s, axis=-1, keepdims=True))
    l = jnp.sum(p, axis=-1, keepdims=True)
    return _dot(p.astype(BF16), v.astype(BF16)) / l


def _cross_prompt_kernel(q_ref, k_ref, v_ref, o_ref, *, heads):
    for h in range(heads):
        sl = slice(h * HEAD, (h + 1) * HEAD)
        o_ref[:, sl] = _cross_head(q_ref[:, sl], k_ref[:, sl], v_ref[:, sl])


def _cross_sample_kernel(q_ref, k_ref, v_ref, o_ref, *, heads, spb, t_new):
    for s in range(spb):
        rows = slice(s * t_new, (s + 1) * t_new)
        for h in range(heads):
            sl = slice(h * HEAD, (h + 1) * HEAD)
            o_ref[rows, sl] = _cross_head(q_ref[rows, sl], k_ref[s, :, h, :], v_ref[s, :, h, :])


def _cross_prompt(qc, mk, mv, seq, heads, bq):
    w = qc.shape[1]
    n = mk.shape[0]
    return pl.pallas_call(
        functools.partial(_cross_prompt_kernel, heads=heads),
        grid=(seq // bq,),
        in_specs=[pl.BlockSpec((bq, w), lambda i: (i, 0)),
                  pl.BlockSpec((n, w), lambda i: (0, 0)),
                  pl.BlockSpec((n, w), lambda i: (0, 0))],
        out_specs=pl.BlockSpec((bq, w), lambda i: (i, 0)),
        out_shape=jax.ShapeDtypeStruct((seq, w), F32),
        compiler_params=_cparams(("parallel",)),
        name="cross_prompt",
    )(qc, mk, mv)


def _cross_sample(qc, mem_k, mem_v, layer, seq, t_new):
    depth, batch, n, heads, _ = mem_k.shape
    w = heads * HEAD
    spb = _pick(batch, (4, 2, 1))
    rows = spb * t_new
    assert seq % rows == 0
    r0 = seq // rows
    mem_spec = pl.BlockSpec((None, spb, n, heads, HEAD), lambda b: (layer, b, 0, 0, 0))
    return pl.pallas_call(
        functools.partial(_cross_sample_kernel, heads=heads, spb=spb, t_new=t_new),
        grid=(batch // spb,),
        in_specs=[pl.BlockSpec((rows, w), lambda b: (r0 + b, 0)), mem_spec, mem_spec],
        out_specs=pl.BlockSpec((rows, w), lambda b: (b, 0)),
        out_shape=jax.ShapeDtypeStruct((batch * t_new, w), F32),
        compiler_params=_cparams(("parallel",)),
        name="cross_sample",
    )(qc, mem_k, mem_v)


def _rope_tables(seq, past_len, batch, t_new):
    half = HEAD // 2
    inv_freq = ROPE_THETA ** (-jnp.arange(half, dtype=F32) / half)
    pos = jnp.concatenate([jnp.arange(seq, dtype=jnp.int32),
                           jnp.tile(past_len + jnp.arange(t_new, dtype=jnp.int32), batch)])
    ang = pos.astype(F32)[:, None] * inv_freq[None, :]
    cos = jnp.cos(ang)
    sin = jnp.sin(ang)
    return jnp.concatenate([cos, cos], axis=-1), jnp.concatenate([-sin, sin], axis=-1)


def kernel(x_prompt, x_sample, cache_k, cache_v, state_hgrn, cache_mem_k, cache_mem_v, page_table, mem_prompt,
           norm_pre, norm_post, ffn_gate, ffn_up, ffn_down, w_in, lower_bounds, hgrn_gnorm,
           w_branch_r, w_branch_a, w_out, mem_norm, w_cq, w_ck, w_cv, w_co):
    b_p, seq, d = x_prompt.shape
    batch, t_new, _ = x_sample.shape
    depth, n_pool, page, h_a, _ = cache_k.shape
    h_r = state_hgrn.shape[2]
    h_m = cache_mem_k.shape[3]
    n_mem = cache_mem_k.shape[2]
    n_pages = page_table.shape[1]
    past_len = n_pages * page
    assert b_p == 1 and h_r == 8 and h_a == 8
    assert seq % MOBA_BLOCK == 0 and past_len % MOBA_BLOCK == 0 and t_new <= MOBA_BLOCK
    assert t_new % 8 == 0 and seq % t_new == 0 and MOBA_BLOCK % page == 0
    m_s = batch * t_new
    m = seq + m_s
    w8 = 8 * HEAD

    bm = _pick(m, (768, 512, 384, 256, 128))
    bm_res = _pick(m_s, (256, 128))
    assert seq % bm_res == 0
    bm_ff = _pick(m, (768, 512, 384, 256, 128))
    bn_ff = _pick(ffn_gate.shape[-1], (512, 256))
    hg_c = _pick(seq, (256,))
    pps = _pick(n_pages, (16, 8, 4, 2))

    cos_t, sin_t = _rope_tables(seq, past_len, batch, t_new)
    lower_bounds = lower_bounds.astype(F32)
    ffn_down = ffn_down.astype(BF16)

    x = (x_prompt.reshape(seq, d), x_sample.reshape(m_s, d))
    u = _norm_cast(x[0], x[1], norm_pre[0, 0][None], bm_res)

    outs = {k: [] for k in ("kp", "vp", "sp", "mkp", "mvp", "ks", "vs", "ss")}
    for l in range(depth):
        g_pre = lambda i: norm_pre[l, i][None]
        g_post = lambda i: norm_post[l, i][None]
        g_next_layer = norm_pre[l + 1, 0][None] if l + 1 < depth else None

        mk_p, mv_p = _memkv(mem_prompt.reshape(n_mem, d), mem_norm[l][None], w_ck, w_cv, l)

        h = _ffn_up(u, ffn_gate, ffn_up, (l, 0), bm_ff, bn_ff)
        x, u = _proj_res(h, ffn_down, (l, 0), x, g_post(0), g_pre(1), 0.5, bm_res, seq)

        pack = _win_hgrn(u, w_in, lower_bounds, l, bm)
        q_a, = _win_rope(u, w_in, l, cos_t, sin_t, 4, h_a, bm, False)
        k_a, k_bf = _win_rope(u, w_in, l, cos_t, sin_t, 5, h_a, bm, True)
        v, vt = _win_v(u, w_in, l, 6, h_a, bm)
        gates = _win_gates(u, w_in, l, 7, 2 * d // w8, bm)

        o_r_p, s_p = _hgrn_prompt(pack, hgrn_gnorm, l, seq, h_r, hg_c, HGRN_HEADS_PER_STEP)
        o_r_s, s_s = _hgrn_sample(pack, hgrn_gnorm, state_hgrn, l, seq, batch, h_r, t_new)

        o_a_p = _moba_prompt(q_a, k_a, k_bf, vt, seq, h_a)
        o_a_s = _moba_sample(cache_k, cache_v, l, page_table, q_a, k_a, v, seq, t_new, pps)

        merged = _merge((o_r_p, o_r_s), (o_a_p, o_a_s), gates, w_branch_r, w_branch_a, l, bm_res, seq)
        x, u = _proj_res(merged, w_out, (l,), x, g_post(1), g_pre(2), 1.0, bm_res, seq)

        qc = _scaled_proj(u, w_cq, (l,), HEAD ** -0.5, bm)
        oc_p = _cross_prompt(qc, mk_p, mv_p, seq, h_m, _pick(seq, (512, 256)))
        oc_s = _cross_sample(qc, cache_mem_k, cache_mem_v, l, seq, t_new)
        x, u = _proj_res((oc_p, oc_s), w_co, (l,), x, g_post(2), g_pre(3), 1.0, bm_res, seq)

        h = _ffn_up(u, ffn_gate, ffn_up, (l, 1), bm_ff, bn_ff)
        res = _proj_res(h, ffn_down, (l, 1), x, g_post(3), g_next_layer, 0.5, bm_res, seq)
        if g_next_layer is None:
            x = tuple(res)
        else:
            x, u = res

        k_all = k_a
        outs["kp"].append(k_all[:seq].reshape(1, seq, h_a, HEAD))
        outs["vp"].append(v[:seq].reshape(1, seq, h_a, HEAD))
        outs["sp"].append(s_p.reshape(1, h_r, HEAD, HEAD))
        outs["mkp"].append(mk_p.reshape(1, n_mem, h_m, HEAD))
        outs["mvp"].append(mv_p.reshape(1, n_mem, h_m, HEAD))
        outs["ks"].append(k_all[seq:].reshape(batch, t_new, h_a, HEAD))
        outs["vs"].append(v[seq:].reshape(batch, t_new, h_a, HEAD))
        outs["ss"].append(s_s)

    st = lambda k: jnp.stack(outs[k])
    return (x[0].reshape(1, seq, d), x[1].reshape(batch, t_new, d),
            st("kp"), st("vp"), st("sp"), st("mkp"), st("mvp"), st("ks"), st("vs"), st("ss"))
```
